```python
import jax, jax.numpy as jnp
from jax import lax
import numpy as np

D_MODEL = 2048
BATCH = 8
SEQ = 2048
DEPTH = 1

N_META = 16
RMS_EPS = 1e-6
HEAD_DIM = 128
ATT_Q_HEADS = 8
ATT_KV_HEADS = 2
ATT_GROUP = ATT_Q_HEADS // ATT_KV_HEADS
WINDOW = 128
ATT_BLOCK = 128
ROPE_THETA = 500000.0
ROPE_DIM = HEAD_DIM // 4
HGRN_HEADS = 8
HGRN_DK = 128
HGRN_DV = 128
HGRN_CHUNK = 64
ATT_WIDTH = ATT_Q_HEADS * HEAD_DIM
HGRN_WIDTH = HGRN_HEADS * HGRN_DV
MIX_WIDTH = ATT_WIDTH + HGRN_WIDTH
Q_COLS = ATT_Q_HEADS * HEAD_DIM
KV_COLS = ATT_KV_HEADS * HEAD_DIM
HK = HGRN_HEADS * HGRN_DK
HV = HGRN_HEADS * HGRN_DV
IN_SPLITS = [Q_COLS, KV_COLS, KV_COLS, HK, HK, HV, HV]
IN_COLS = sum(IN_SPLITS)
PEER_HEADS = 8
PEER_NKEYS = 128
PEER_N = PEER_NKEYS * PEER_NKEYS
PEER_DKEY = 256
PEER_DHALF = PEER_DKEY // 2
PEER_TOPK = 16
PEER_TOKEN_BLOCK = 128

kernel_name = "hymba_swa_sink_hgrn2_peer_block"


def rms_norm(x, g):
    xf = x.astype(jnp.float32)
    y = xf * lax.rsqrt(jnp.mean(xf * xf, axis=-1, keepdims=True) + RMS_EPS)
    return (y * g.astype(jnp.float32)).astype(x.dtype)


def partial_rope(x, pos):
    half = ROPE_DIM // 2
    inv = ROPE_THETA ** (-jnp.arange(0, ROPE_DIM, 2, dtype=jnp.float32) / ROPE_DIM)
    ang = pos.astype(jnp.float32)[:, None] * inv[None, :]
    cos = jnp.cos(ang)[None, :, None, :]
    sin = jnp.sin(ang)[None, :, None, :]
    xr = x[..., :ROPE_DIM].astype(jnp.float32)
    x1, x2 = xr[..., :half], xr[..., half:]
    rot = jnp.concatenate([x1 * cos - x2 * sin, x2 * cos + x1 * sin], axis=-1)
    return jnp.concatenate([rot.astype(x.dtype), x[..., ROPE_DIM:]], axis=-1)


def sliding_window_sink_attention(q, k, v, sinks):
    B, L = q.shape[0], q.shape[1]
    pad = ATT_BLOCK - N_META
    Lp = L + pad
    nb = Lp // ATT_BLOCK
    padw = ((0, 0), (pad, 0), (0, 0), (0, 0))
    qb = jnp.pad(q, padw).reshape(B, nb, ATT_BLOCK, ATT_KV_HEADS, ATT_GROUP, HEAD_DIM)
    kb = jnp.pad(k, padw).reshape(B, nb, ATT_BLOCK, ATT_KV_HEADS, HEAD_DIM)
    vb = jnp.pad(v, padw).reshape(B, nb, ATT_BLOCK, ATT_KV_HEADS, HEAD_DIM)

    def with_meta_and_band(t, tb):
        prev = jnp.concatenate([jnp.zeros_like(tb[:, :1]), tb[:, :-1]], axis=1)
        meta = jnp.broadcast_to(t[:, None, :N_META], (B, nb, N_META, ATT_KV_HEADS, HEAD_DIM))
        return jnp.concatenate([meta, prev, tb], axis=2)

    keys = with_meta_and_band(k, kb)
    vals = with_meta_and_band(v, vb)
    qpos = (jnp.arange(Lp) - pad).reshape(nb, ATT_BLOCK)
    kpos = jnp.concatenate([qpos - ATT_BLOCK, qpos], axis=-1)
    rel = qpos[:, :, None] - kpos[:, None, :]
    band_mask = (rel >= 0) & (rel < WINDOW) & (kpos[:, None, :] >= N_META)
    meta_mask = jnp.arange(N_META)[None, None, :] <= qpos[:, :, None]
    mask = jnp.concatenate([meta_mask, band_mask], axis=-1)

    scale = HEAD_DIM ** -0.5
    scores = jnp.einsum('bnqhgd,bnkhd->bhgnqk', qb, keys).astype(jnp.float32) * scale
    scores = jnp.where(mask, scores, -jnp.inf)
    sink = jnp.broadcast_to(sinks.astype(jnp.float32).reshape(1, ATT_KV_HEADS, ATT_GROUP, 1, 1, 1),
                            scores.shape[:-1] + (1,))
    probs = jax.nn.softmax(jnp.concatenate([scores, sink], axis=-1), axis=-1)[..., :-1]
    out = jnp.einsum('bhgnqk,bnkhd->bnqhgd', probs.astype(v.dtype), vals)
    return out.reshape(B, Lp, ATT_WIDTH)[:, pad:]


def _hgrn2_chunk_step(state, inp):
    q, lf, kk, v = inp
    C = q.shape[2]
    b = jnp.cumsum(lf, axis=2)
    causal = jnp.tril(jnp.ones((C, C), dtype=bool))
    diff = b[:, :, :, None, :] - b[:, :, None, :, :]
    decay = jnp.exp(jnp.where(causal[None, None, :, :, None], diff, -jnp.inf))
    scores = jnp.einsum('bhtsk,bhsk->bhts', q[:, :, :, None, :] * decay, kk)
    o = (jnp.einsum('bhts,bhsv->bhtv', scores, v)
         + jnp.einsum('bhtk,bhkv->bhtv', q * jnp.exp(b), state))
    b_last = b[:, :, -1:, :]
    new_state = (jnp.exp(b_last[:, :, 0, :])[..., None] * state
                 + jnp.einsum('bhsk,bhsv->bhkv', kk * jnp.exp(b_last - b), v))
    return new_state, o


def hgrn2_mixer(hq, hf, hi, hg, lb, norm_g):
    B, L = hq.shape[0], hq.shape[1]
    out_dtype = hi.dtype
    lbf = lb.astype(jnp.float32)
    z = hf.astype(jnp.float32)
    logf = jnp.log(lbf + (1.0 - lbf) * jax.nn.sigmoid(z))
    kgate = (1.0 - lbf) * jax.nn.sigmoid(-z)
    q = jax.nn.silu(hq.astype(jnp.float32))
    v = hi.astype(jnp.float32)
    pad = HGRN_CHUNK - N_META
    Lp = L + pad
    nc = Lp // HGRN_CHUNK

    def chunks(t, d):
        t = jnp.pad(t, ((0, 0), (pad, 0), (0, 0)))
        return t.reshape(B, nc, HGRN_CHUNK, HGRN_HEADS, d).transpose(1, 0, 3, 2, 4)

    xs = (chunks(q, HGRN_DK), chunks(logf, HGRN_DK), chunks(kgate, HGRN_DK), chunks(v, HGRN_DV))
    s0 = jnp.zeros((B, HGRN_HEADS, HGRN_DK, HGRN_DV), jnp.float32)
    _, o = lax.scan(_hgrn2_chunk_step, s0, xs)
    o = o.transpose(1, 0, 3, 2, 4).reshape(B, Lp, HGRN_HEADS, HGRN_DV)[:, pad:]
    o = rms_norm(o, norm_g.reshape(HGRN_HEADS, HGRN_DV)).reshape(B, L, HV)
    return (o * jax.nn.silu(hg.astype(jnp.float32))).astype(out_dtype)


def peer_ffn(x, w_q, sub_keys, u, v):
    B, S, D = x.shape
    T = B * S
    xt = x.reshape(T, D)
    q = (xt @ w_q).reshape(T, PEER_HEADS, 2, PEER_DHALF)
    s = jnp.einsum('thcd,hcnd->thcn', q, sub_keys).astype(jnp.float32)
    s1v, s1i = lax.top_k(s[:, :, 0], PEER_TOPK)
    s2v, s2i = lax.top_k(s[:, :, 1], PEER_TOPK)
    cand = (s1v[..., :, None] + s2v[..., None, :]).reshape(T, PEER_HEADS, PEER_TOPK * PEER_TOPK)
    cidx = (s1i[..., :, None] * PEER_NKEYS + s2i[..., None, :]).reshape(T, PEER_HEADS, PEER_TOPK * PEER_TOPK)
    top_s, top_pos = lax.top_k(cand, PEER_TOPK)
    eidx = jnp.take_along_axis(cidx, top_pos, axis=-1)
    gates = jax.nn.softmax(top_s, axis=-1)
    nblk = T // PEER_TOKEN_BLOCK
    xb = xt.reshape(nblk, PEER_TOKEN_BLOCK, D)
    eb = eidx.reshape(nblk, PEER_TOKEN_BLOCK, PEER_HEADS * PEER_TOPK)
    gb = gates.reshape(nblk, PEER_TOKEN_BLOCK, PEER_HEADS * PEER_TOPK)

    def block(args):
        xs, es, gs = args
        a = jnp.einsum('td,ted->te', xs, u[es])
        hact = (jax.nn.gelu(a.astype(jnp.float32)) * gs).astype(xs.dtype)
        return jnp.einsum('te,ted->td', hact, v[es])

    out = lax.map(block, (xb, eb, gb))
    return out.reshape(B, S, D)


def setup_inputs(seed: int = 0) -> dict:
    key = jax.random.key(seed)
    ks = jax.random.split(key, 16)
    f32 = jnp.float32

    def nrm(k, shape, scale):
        return jax.random.normal(k, shape, f32) * scale

    return {
        "x": nrm(ks[0], (BATCH, SEQ, D_MODEL), 1.0),
        "meta_tokens": nrm(ks[1], (N_META, D_MODEL), 1.0),
        "hgrn_lb_logits": nrm(ks[2], (DEPTH + 1, HK), 0.5),
        "ln_mix_g": 1.0 + nrm(ks[3], (DEPTH, D_MODEL), 0.02),
        "w_in": nrm(ks[4], (DEPTH, D_MODEL, IN_COLS), D_MODEL ** -0.5),
        "q_norm_g": 1.0 + nrm(ks[5], (DEPTH, HEAD_DIM), 0.02),
        "k_norm_g": 1.0 + nrm(ks[6], (DEPTH, HEAD_DIM), 0.02),
        "attn_sinks": nrm(ks[7], (DEPTH, ATT_Q_HEADS), 0.5),
        "hgrn_norm_g": 1.0 + nrm(ks[8], (DEPTH, HV), 0.02),
        "w_out": nrm(ks[9], (DEPTH, MIX_WIDTH, D_MODEL), MIX_WIDTH ** -0.5),
        "ln_ffn_g": 1.0 + nrm(ks[10], (DEPTH, D_MODEL), 0.02),
        "peer_w_q": nrm(ks[11], (DEPTH, D_MODEL, PEER_HEADS * PEER_DKEY), D_MODEL ** -0.5),
        "peer_sub_keys": nrm(ks[12], (DEPTH, PEER_HEADS, 2, PEER_NKEYS, PEER_DHALF), PEER_DHALF ** -0.5),
        "peer_u": nrm(ks[13], (DEPTH, PEER_N, D_MODEL), D_MODEL ** -0.5),
        "peer_v": nrm(ks[14], (DEPTH, PEER_N, D_MODEL), 0.25),
    }


def reference(x, meta_tokens, hgrn_lb_logits, ln_mix_g, w_in, q_norm_g, k_norm_g, attn_sinks,
              hgrn_norm_g, w_out, ln_ffn_g, peer_w_q, peer_sub_keys, peer_u, peer_v):
    B, S, D = x.shape
    meta = jnp.broadcast_to(meta_tokens[None].astype(x.dtype), (B, N_META, D))
    h = jnp.concatenate([meta, x], axis=1)
    L = S + N_META
    pos = jnp.arange(L)
    lbs = jnp.cumsum(jax.nn.softmax(hgrn_lb_logits.astype(jnp.float32), axis=0), axis=0)
    split_at = list(np.cumsum(IN_SPLITS)[:-1])
    for l in range(DEPTH):
        hn = rms_norm(h, ln_mix_g[l])
        proj = hn @ w_in[l]
        aq, ak, av, hq, hf, hi, hg = jnp.split(proj, split_at, axis=-1)
        aq = partial_rope(rms_norm(aq.reshape(B, L, ATT_Q_HEADS, HEAD_DIM), q_norm_g[l]), pos)
        ak = partial_rope(rms_norm(ak.reshape(B, L, ATT_KV_HEADS, HEAD_DIM), k_norm_g[l]), pos)
        av = av.reshape(B, L, ATT_KV_HEADS, HEAD_DIM)
        att = sliding_window_sink_attention(aq, ak, av, attn_sinks[l])
        rec = hgrn2_mixer(hq, hf, hi, hg, lbs[l], hgrn_norm_g[l])
        h = h + jnp.concatenate([att, rec], axis=-1) @ w_out[l]
        if l == DEPTH - 1:
            h = h[:, N_META:]
        h = h + peer_ffn(rms_norm(h, ln_ffn_g[l]), peer_w_q[l], peer_sub_keys[l], peer_u[l], peer_v[l])
    return h
```

```python
import functools
import math

import jax
import jax.numpy as jnp
import numpy as np
from jax import lax
from jax.experimental import pallas as pl
from jax.experimental.pallas import tpu as pltpu

F32 = jnp.float32
BF16 = jnp.bfloat16

N_META = 16
RMS_EPS = 1e-6
HEAD_DIM = 128
ATT_Q_HEADS = 8
ATT_KV_HEADS = 2
ATT_GROUP = ATT_Q_HEADS // ATT_KV_HEADS
ATT_BLOCK = 128
ROPE_THETA = 500000.0
ROPE_DIM = HEAD_DIM // 4
HGRN_HEADS = 8
HGRN_DK = 128
HGRN_CHUNK = 64
Q_COLS = ATT_Q_HEADS * HEAD_DIM
KV_COLS = ATT_KV_HEADS * HEAD_DIM
HK = HGRN_HEADS * HGRN_DK
PEER_HEADS = 8
PEER_NKEYS = 128
PEER_DHALF = 128
PEER_TOPK = 16

V7X_VMEM_BYTES = 64 * 1024 * 1024
VMEM_LIMIT = V7X_VMEM_BYTES - 8 * 1024 * 1024
LANES = 128
HGRN_EXP_CLAMP = 85.0
MASKED = -1e30


def _params(*sem):
    return pltpu.CompilerParams(dimension_semantics=sem, vmem_limit_bytes=VMEM_LIMIT)


def _rms(x, g):
    ms = jnp.mean(x * x, axis=-1, keepdims=True)
    return x * lax.rsqrt(ms + RMS_EPS) * g


def _sigmoid(x):
    return 1.0 / (1.0 + jnp.exp(-x))


def _in_proj_kernel(x_ref, g_ref, w_ref, o_ref, xn_ref):
    @pl.when(pl.program_id(1) == 0)
    def _():
        xn_ref[...] = _rms(x_ref[...], g_ref[...]).astype(BF16)

    o_ref[...] = jnp.dot(xn_ref[...], w_ref[...], preferred_element_type=F32)


def _in_proj(x2d, g, w, tm, tn):
    m, d = x2d.shape
    n = w.shape[1]
    return pl.pallas_call(
        _in_proj_kernel,
        grid=(m // tm, n // tn),
        in_specs=[
            pl.BlockSpec((tm, d), lambda i, j: (i, 0)),
            pl.BlockSpec((1, d), lambda i, j: (0, 0)),
            pl.BlockSpec((d, tn), lambda i, j: (0, j)),
        ],
        out_specs=pl.BlockSpec((tm, tn), lambda i, j: (i, j)),
        out_shape=jax.ShapeDtypeStruct((m, n), F32),
        scratch_shapes=[pltpu.VMEM((tm, d), BF16)],
        compiler_params=_params("arbitrary", "arbitrary"),
        name="in_proj",
    )(x2d, g.reshape(1, d), w)


def _rope(x, tab_ref):
    return (x * tab_ref[0]
            + pltpu.roll(x, HEAD_DIM - ROPE_DIM // 2, 1) * tab_ref[1]
            + pltpu.roll(x, ROPE_DIM // 2, 1) * tab_ref[2])


def _attn_kernel(sink_ref, q_ref, kp_ref, ko_ref, vp_ref, vo_ref, km_ref, vm_ref,
                 qg_ref, kg_ref, tab_ref, tabp_ref, tabm_ref, o_ref):
    n = pl.program_id(1)
    blk = ATT_BLOCK
    rows = ATT_GROUP * blk
    scale = HEAD_DIM ** -0.5
    qi = lax.broadcasted_iota(jnp.int32, (rows, blk), 0) & (blk - 1)
    ki = lax.broadcasted_iota(jnp.int32, (rows, blk), 1)
    own_mask = ki <= qi
    prev_mask = ki > jnp.where(n > 0, qi, blk)
    head_row = lax.shift_right_logical(lax.broadcasted_iota(jnp.int32, (rows, 1), 0), 7)
    nt = (((1,), (1,)), ((), ()))
    for g in range(ATT_KV_HEADS):
        ksl = slice(g * HEAD_DIM, (g + 1) * HEAD_DIM)
        k_own = _rope(_rms(ko_ref[:, ksl], kg_ref[...]), tab_ref).astype(BF16)
        k_prev = _rope(_rms(kp_ref[:, ksl], kg_ref[...]), tabp_ref).astype(BF16)
        k_meta = _rope(_rms(km_ref[:, ksl], kg_ref[...]), tabm_ref).astype(BF16)
        qs = []
        sink = jnp.zeros((rows, 1), F32)
        for hh in range(ATT_GROUP):
            h = g * ATT_GROUP + hh
            qh = q_ref[:, h * HEAD_DIM:(h + 1) * HEAD_DIM]
            qs.append(_rope(_rms(qh, qg_ref[...]), tab_ref))
            sink = jnp.where(head_row == hh, sink_ref[h], sink)
        q = jnp.concatenate(qs, axis=0).astype(BF16)
        s_own = lax.dot_general(q, k_own, nt, preferred_element_type=F32) * scale
        s_prev = lax.dot_general(q, k_prev, nt, preferred_element_type=F32) * scale
        s_meta = lax.dot_general(q, k_meta, nt, preferred_element_type=F32) * scale
        s_own = jnp.where(own_mask, s_own, MASKED)
        s_prev = jnp.where(prev_mask, s_prev, MASKED)
        m = jnp.maximum(jnp.maximum(jnp.max(s_own, axis=-1, keepdims=True),
                                    jnp.max(s_prev, axis=-1, keepdims=True)),
                        jnp.maximum(jnp.max(s_meta, axis=-1, keepdims=True), sink))
        p_own = jnp.exp(s_own - m)
        p_prev = jnp.exp(s_prev - m)
        p_meta = jnp.exp(s_meta - m)
        denom = (jnp.sum(p_own, axis=-1, keepdims=True) + jnp.sum(p_prev, axis=-1, keepdims=True)
                 + jnp.sum(p_meta, axis=-1, keepdims=True) + jnp.exp(sink - m))
        acc = (jnp.dot(p_own.astype(BF16), vo_ref[:, ksl].astype(BF16), preferred_element_type=F32)
               + jnp.dot(p_prev.astype(BF16), vp_ref[:, ksl].astype(BF16), preferred_element_type=F32)
               + jnp.dot(p_meta.astype(BF16), vm_ref[:, ksl].astype(BF16), preferred_element_type=F32))
        out = acc / denom
        for hh in range(ATT_GROUP):
            h = g * ATT_GROUP + hh
            o_ref[:, h * HEAD_DIM:(h + 1) * HEAD_DIM] = out[hh * blk:(hh + 1) * blk].astype(BF16)


def _rope_tables(pos):
    half = ROPE_DIM // 2
    inv = ROPE_THETA ** (-jnp.arange(0, ROPE_DIM, 2, dtype=F32) / ROPE_DIM)
    ang = pos.astype(F32)[:, None] * inv[None, :]
    cos, sin = jnp.cos(ang), jnp.sin(ang)
    n = pos.shape[0]
    ones = jnp.ones((n, HEAD_DIM - ROPE_DIM), F32)
    zeros = jnp.zeros((n, HEAD_DIM - half), F32)
    c = jnp.concatenate([cos, cos, ones], axis=1)
    s1 = jnp.concatenate([-sin, zeros], axis=1)
    s2 = jnp.concatenate([jnp.zeros((n, half), F32), sin, jnp.zeros((n, HEAD_DIM - ROPE_DIM), F32)], axis=1)
    return jnp.stack([c, s1, s2])


def _attention(proj, proj_meta, sinks, qg, kg, batch, seq, col_q, col_k, col_v):
    nb = seq // ATT_BLOCK
    t = batch * seq
    tab = _rope_tables(jnp.arange(seq) + N_META)
    tabm = _rope_tables(jnp.arange(N_META))
    kb, vb = col_k // KV_COLS, col_v // KV_COLS

    def own(b, n):
        return b * nb + n

    def prev(b, n):
        return b * nb + jnp.maximum(n - 1, 0)

    return pl.pallas_call(
        _attn_kernel,
        grid=(batch, nb),
        in_specs=[
            pl.BlockSpec(memory_space=pltpu.SMEM),
            pl.BlockSpec((ATT_BLOCK, Q_COLS), lambda b, n: (own(b, n), col_q // Q_COLS)),
            pl.BlockSpec((ATT_BLOCK, KV_COLS), lambda b, n: (prev(b, n), kb)),
            pl.BlockSpec((ATT_BLOCK, KV_COLS), lambda b, n: (own(b, n), kb)),
            pl.BlockSpec((ATT_BLOCK, KV_COLS), lambda b, n: (prev(b, n), vb)),
            pl.BlockSpec((ATT_BLOCK, KV_COLS), lambda b, n: (own(b, n), vb)),
            pl.BlockSpec((N_META, KV_COLS), lambda b, n: (0, kb)),
            pl.BlockSpec((N_META, KV_COLS), lambda b, n: (0, vb)),
            pl.BlockSpec((1, HEAD_DIM), lambda b, n: (0, 0)),
            pl.BlockSpec((1, HEAD_DIM), lambda b, n: (0, 0)),
            pl.BlockSpec((3, ATT_BLOCK, HEAD_DIM), lambda b, n: (0, n, 0)),
            pl.BlockSpec((3, ATT_BLOCK, HEAD_DIM), lambda b, n: (0, jnp.maximum(n - 1, 0), 0)),
            pl.BlockSpec((3, N_META, HEAD_DIM), lambda b, n: (0, 0, 0)),
        ],
        out_specs=pl.BlockSpec((ATT_BLOCK, Q_COLS), lambda b, n: (own(b, n), 0)),
        out_shape=jax.ShapeDtypeStruct((t, Q_COLS), BF16),
        compiler_params=_params("arbitrary", "arbitrary"),
        name="swa_attention",
    )(sinks, proj, proj, proj, proj, proj, proj_meta, proj_meta,
      qg.reshape(1, HEAD_DIM), kg.reshape(1, HEAD_DIM), tab, tab, tabm)


def _cumsum_rows(x):
    n = x.shape[0]
    row = lax.broadcasted_iota(jnp.int32, x.shape, 0)
    shift = 1
    while shift < n:
        x = x + jnp.where(row >= shift, pltpu.roll(x, shift, 0), 0.0)
        shift *= 2
    return x


def _hgrn_chunk(hq, hf, hi, hg, lb, ng, st_ref, valid):
    c = hq.shape[0]
    f = lb + (1.0 - lb) * _sigmoid(hf)
    logf = jnp.log(f)
    kk = (1.0 - lb) * _sigmoid(-hf)
    if valid is not None:
        logf = jnp.where(valid, logf, 0.0)
        kk = jnp.where(valid, kk, 0.0)
    q = hq * _sigmoid(hq)
    b = _cumsum_rows(logf)
    b_last = b[c - 1:c, :]
    b_mid = b[c // 2 - 1:c // 2, :]
    q_mid = (q * jnp.exp(jnp.minimum(b - b_mid, HGRN_EXP_CLAMP))).astype(BF16)
    k_mid = (kk * jnp.exp(jnp.minimum(b_mid - b, HGRN_EXP_CLAMP))).astype(BF16)
    q_in = (q * jnp.exp(b)).astype(BF16)
    k_out = (kk * jnp.exp(b_last - b)).astype(BF16)
    decay = jnp.exp(b_last)
    v = hi.astype(BF16)
    causal = (lax.broadcasted_iota(jnp.int32, (c, c), 0) >= lax.broadcasted_iota(jnp.int32, (c, c), 1))
    nt = (((1,), (1,)), ((), ()))
    tn = (((0,), (0,)), ((), ()))
    outs = []
    for h in range(HGRN_HEADS):
        sl = slice(h * HGRN_DK, (h + 1) * HGRN_DK)
        st = st_ref[h]
        scores = lax.dot_general(q_mid[:, sl], k_mid[:, sl], nt, preferred_element_type=F32)
        scores = jnp.where(causal, scores, 0.0).astype(BF16)
        o = (jnp.dot(scores, v[:, sl], preferred_element_type=F32)
             + lax.dot_general(q_in[:, sl], st.astype(BF16), nt, preferred_element_type=F32))
        st_ref[h] = (st * decay[:, sl]
                     + lax.dot_general(v[:, sl], k_out[:, sl], tn, preferred_element_type=F32))
        outs.append(_rms(o, ng[:, sl]))
    o_all = jnp.concatenate(outs, axis=1)
    return o_all * (hg * _sigmoid(hg))


def _lower_bound(lbl_ref):
    l0, l1 = lbl_ref[0:1, :], lbl_ref[1:2, :]
    m = jnp.maximum(l0, l1)
    e0, e1 = jnp.exp(l0 - m), jnp.exp(l1 - m)
    return e0 / (e0 + e1)


def _hgrn_meta_kernel(hq_ref, hf_ref, hi_ref, hg_ref, lbl_ref, ng_ref, st_out_ref):
    st_out_ref[...] = jnp.zeros_like(st_out_ref)
    _hgrn_chunk(hq_ref[...], hf_ref[...], hi_ref[...], hg_ref[...], _lower_bound(lbl_ref), ng_ref[...],
                st_out_ref, None)


def _hgrn_kernel(hq_ref, hf_ref, hi_ref, hg_ref, lbl_ref, ng_ref, st0_ref, o_ref, st_ref, *, chunks):
    @pl.when(pl.program_id(1) == 0)
    def _():
        st_ref[...] = st0_ref[...]

    lb = _lower_bound(lbl_ref)

    def body(ci, carry):
        rows = pl.ds(pl.multiple_of(ci * HGRN_CHUNK, HGRN_CHUNK), HGRN_CHUNK)
        o = _hgrn_chunk(hq_ref[rows, :], hf_ref[rows, :], hi_ref[rows, :], hg_ref[rows, :], lb, ng_ref[...],
                        st_ref, None)
        o_ref[rows, :] = o.astype(BF16)
        return carry

    lax.fori_loop(0, chunks, body, 0)


def _hgrn(proj, proj_meta, lb_logits, norm_g, batch, seq, col_hq, rows_per_step):
    t = batch * seq
    cb = col_hq // HK
    state_shape = (HGRN_HEADS, HGRN_DK, HGRN_DK)
    vec = pl.BlockSpec((1, HK), lambda *_: (0, 0))
    lbs = pl.BlockSpec((2, HK), lambda *_: (0, 0))
    st_meta = pl.pallas_call(
        _hgrn_meta_kernel,
        grid=(1,),
        in_specs=[pl.BlockSpec((N_META, HK), lambda i, k=k: (0, cb + k)) for k in range(4)] + [lbs, vec],
        out_specs=pl.BlockSpec(state_shape, lambda i: (0, 0, 0)),
        out_shape=jax.ShapeDtypeStruct(state_shape, F32),
        compiler_params=_params("arbitrary"),
        name="hgrn2_meta",
    )(proj_meta, proj_meta, proj_meta, proj_meta, lb_logits, norm_g.reshape(1, HK))
    steps = seq // rows_per_step
    return pl.pallas_call(
        functools.partial(_hgrn_kernel, chunks=rows_per_step // HGRN_CHUNK),
        grid=(batch, steps),
        in_specs=[pl.BlockSpec((rows_per_step, HK), lambda b, s, k=k: (b * steps + s, cb + k)) for k in range(4)]
        + [lbs, vec, pl.BlockSpec(state_shape, lambda b, s: (0, 0, 0))],
        out_specs=pl.BlockSpec((rows_per_step, HK), lambda b, s: (b * steps + s, 0)),
        out_shape=jax.ShapeDtypeStruct((t, HK), BF16),
        scratch_shapes=[pltpu.VMEM(state_shape, F32)],
        compiler_params=_params("arbitrary", "arbitrary"),
        name="hgrn2",
    )(proj, proj, proj, proj, lb_logits, norm_g.reshape(1, HK), st_meta)


def _out_proj_kernel(x_ref, att_ref, rec_ref, wa_ref, wr_ref, g_ref, h_ref, xnt_ref):
    h = (x_ref[...]
         + jnp.dot(att_ref[...], wa_ref[...], preferred_element_type=F32)
         + jnp.dot(rec_ref[...], wr_ref[...], preferred_element_type=F32))
    h_ref[...] = h
    xnt_ref[...] = jnp.transpose(_rms(h, g_ref[...])).astype(BF16)


def _out_proj(x2d, att, rec, wa, wr, g, tm):
    t, d = x2d.shape
    return pl.pallas_call(
        _out_proj_kernel,
        grid=(t // tm,),
        in_specs=[
            pl.BlockSpec((tm, d), lambda i: (i, 0)),
            pl.BlockSpec((tm, att.shape[1]), lambda i: (i, 0)),
            pl.BlockSpec((tm, rec.shape[1]), lambda i: (i, 0)),
            pl.BlockSpec(wa.shape, lambda i: (0, 0)),
            pl.BlockSpec(wr.shape, lambda i: (0, 0)),
            pl.BlockSpec((1, d), lambda i: (0, 0)),
        ],
        out_specs=[pl.BlockSpec((tm, d), lambda i: (i, 0)), pl.BlockSpec((d, tm), lambda i: (0, i))],
        out_shape=[jax.ShapeDtypeStruct((t, d), F32), jax.ShapeDtypeStruct((d, t), BF16)],
        compiler_params=_params("arbitrary"),
        name="out_proj",
    )(x2d, att, rec, wa, wr, g.reshape(1, d))


def _staircase():
    return [(r1, PEER_TOPK // (r1 + 1)) for r1 in range(PEER_TOPK)]


def _extract_desc(x, rounds, collect_rows):
    vals = []
    for _ in range(rounds):
        m = jnp.max(x, axis=0, keepdims=True)
        vals.append(jnp.maximum(m, 0.0))
        x = jnp.where(x == m, -1.0, x)
    if not collect_rows:
        return vals, None
    row = lax.broadcasted_iota(jnp.int32, (rounds, x.shape[1]), 0)
    arr = jnp.zeros((rounds, x.shape[1]), F32)
    for r, m in enumerate(vals):
        arr = jnp.where(row == r, m, arr)
    return vals, arr


def _candidates(w1_rows, w1_arr, w2_arr):
    lanes = w2_arr.shape[1]
    row8 = lax.broadcasted_iota(jnp.int32, (8, lanes), 0)
    pieces = [w1_rows[0] * w2_arr[0:8], w1_rows[0] * w2_arr[8:16]]
    for r1, cnt in _staircase()[1:8]:
        pieces.append(jnp.where(row8 < cnt, w1_rows[r1] * w2_arr[0:8], -1.0))
    pieces.append(w1_arr[8:16] * w2_arr[0:1])
    return jnp.concatenate(pieces, axis=0)


def _peer_keys_kernel(xt_ref, wqt_ref, keys_ref, a1_ref, a2_ref, tau_ref, q_s, s_s):
    tm = xt_ref.shape[1]
    q_s[...] = jnp.dot(wqt_ref[...], xt_ref[...], preferred_element_type=F32).astype(BF16)
    for h in range(PEER_HEADS):
        for c in range(2):
            r0 = (h * 2 + c) * PEER_DHALF
            s_s[h, c] = jnp.dot(keys_ref[h, c], q_s[r0:r0 + PEER_DHALF, :], preferred_element_type=F32)

    nchunk = tm // LANES

    def body(it, carry):
        h = it // nchunk
        lanes = pl.ds(pl.multiple_of((it % nchunk) * LANES, LANES), LANES)
        s1 = s_s[h, 0, :, lanes]
        s2 = s_s[h, 1, :, lanes]
        e1 = jnp.exp(s1 - jnp.max(s1, axis=0, keepdims=True))
        e2 = jnp.exp(s2 - jnp.max(s2, axis=0, keepdims=True))
        w1_rows, w1_arr = _extract_desc(e1, PEER_TOPK, True)
        _, w2_arr = _extract_desc(e2, PEER_TOPK, True)
        top, _ = _extract_desc(_candidates(w1_rows, w1_arr, w2_arr), PEER_TOPK, False)
        z = top[0]
        for m in top[1:]:
            z = z + m
        inv_z = 1.0 / z
        w1n_rows = [w * inv_z for w in w1_rows]
        top_n, _ = _extract_desc(_candidates(w1n_rows, w1_arr * inv_z, w2_arr), PEER_TOPK, False)
        a1_ref[h, :, lanes] = e1 * inv_z
        a2_ref[h, :, lanes] = e2
        tau_ref[h, :, lanes] = top_n[PEER_TOPK - 1]
        return carry

    lax.fori_loop(0, PEER_HEADS * nchunk, body, 0)


def _peer_keys(xnt, wqt, keys, tm):
    d, t = xnt.shape
    dq = wqt.shape[0]
    fac = jax.ShapeDtypeStruct((PEER_HEADS, PEER_NKEYS, t), F32)
    fac_spec = pl.BlockSpec((PEER_HEADS, PEER_NKEYS, tm), lambda i: (0, 0, i))
    return pl.pallas_call(
        _peer_keys_kernel,
        grid=(t // tm,),
        in_specs=[
            pl.BlockSpec((d, tm), lambda i: (0, i)),
            pl.BlockSpec((dq, d), lambda i: (0, 0)),
            pl.BlockSpec(keys.shape, lambda i: (0, 0, 0, 0)),
        ],
        out_specs=[fac_spec, fac_spec, pl.BlockSpec((PEER_HEADS, 1, tm), lambda i: (0, 0, i))],
        out_shape=[fac, fac, jax.ShapeDtypeStruct((PEER_HEADS, 1, t), F32)],
        scratch_shapes=[pltpu.VMEM((dq, tm), BF16), pltpu.VMEM((PEER_HEADS, 2, PEER_NKEYS, tm), F32)],
        compiler_params=_params("arbitrary"),
        name="peer_keys",
    )(xnt, wqt, keys)


def _gelu_tanh(x):
    c = math.sqrt(2.0 / math.pi)
    return x * (0.5 * (1.0 + jnp.tanh(c * (x + 0.044715 * (x * x * x)))))


def _peer_dense_kernel(xt_ref, u_ref, vt_ref, a1_ref, a2_ref, tau_ref, o_ref, a_s, h_s, *, tile_rows, tile_lanes):
    e = pl.program_id(1)
    eb, tm = a_s.shape
    keys_per_step = eb // PEER_NKEYS

    @pl.when(e == 0)
    def _():
        o_ref[...] = jnp.zeros_like(o_ref)

    a_s[...] = jnp.dot(u_ref[...], xt_ref[...], preferred_element_type=F32)

    n_r = PEER_NKEYS // tile_rows
    n_l = tm // tile_lanes

    def body(it, carry):
        k = it // (n_r * n_l)
        rem = it % (n_r * n_l)
        r0 = pl.multiple_of((rem // n_l) * tile_rows, tile_rows)
        lanes = pl.ds(pl.multiple_of((rem % n_l) * tile_lanes, tile_lanes), tile_lanes)
        i1 = e * keys_per_step + k
        gate = jnp.zeros((tile_rows, tile_lanes), F32)
        for h in range(PEER_HEADS):
            p = a1_ref[h, pl.ds(i1, 1), lanes] * a2_ref[h, pl.ds(r0, tile_rows), lanes]
            gate = gate + jnp.where(p >= tau_ref[h, :, lanes], p, 0.0)
        rows = pl.ds(pl.multiple_of(k * PEER_NKEYS + r0, tile_rows), tile_rows)
        h_s[rows, lanes] = (_gelu_tanh(a_s[rows, lanes]) * gate).astype(BF16)
        return carry

    lax.fori_loop(0, keys_per_step * n_r * n_l, body, 0)
    o_ref[...] += jnp.dot(vt_ref[...], h_s[...], preferred_element_type=F32)


def _peer_dense(xnt, u, vt, a1, a2, tau, tm, eb):
    d, t = xnt.shape
    n_exp = u.shape[0]
    fac_spec = pl.BlockSpec((PEER_HEADS, PEER_NKEYS, tm), lambda i, e: (0, 0, i))
    return pl.pallas_call(
        functools.partial(_peer_dense_kernel, tile_rows=64, tile_lanes=min(tm, 256)),
        grid=(t // tm, n_exp // eb),
        in_specs=[
            pl.BlockSpec((d, tm), lambda i, e: (0, i)),
            pl.BlockSpec((eb, d), lambda i, e: (e, 0)),
            pl.BlockSpec((d, eb), lambda i, e: (0, e)),
            fac_spec, fac_spec,
            pl.BlockSpec((PEER_HEADS, 1, tm), lambda i, e: (0, 0, i)),
        ],
        out_specs=pl.BlockSpec((d, tm), lambda i, e: (0, i)),
        out_shape=jax.ShapeDtypeStruct((d, t), F32),
        scratch_shapes=[pltpu.VMEM((eb, tm), F32), pltpu.VMEM((eb, tm), BF16)],
        compiler_params=_params("arbitrary", "arbitrary"),
        name="peer_dense",
    )(xnt, u, vt, a1, a2, tau)


def _finish_kernel(h_ref, pt_ref, o_ref):
    o_ref[...] = h_ref[...] + jnp.transpose(pt_ref[...])


def _finish(h2, peer_t, tm):
    t, d = h2.shape
    return pl.pallas_call(
        _finish_kernel,
        grid=(t // tm,),
        in_specs=[pl.BlockSpec((tm, d), lambda i: (i, 0)), pl.BlockSpec((d, tm), lambda i: (0, i))],
        out_specs=pl.BlockSpec((tm, d), lambda i: (i, 0)),
        out_shape=jax.ShapeDtypeStruct((t, d), F32),
        compiler_params=_params("arbitrary"),
        name="finish",
    )(h2, peer_t)


def _tile(n, want):
    t = min(n, want)
    assert n % t == 0, (n, want)
    return t


def kernel(x, meta_tokens, hgrn_lb_logits, ln_mix_g, w_in, q_norm_g, k_norm_g, attn_sinks, hgrn_norm_g,
           w_out, ln_ffn_g, peer_w_q, peer_sub_keys, peer_u, peer_v):
    batch, seq, d = x.shape
    assert w_in.shape[0] == 1 and hgrn_lb_logits.shape[0] == 2, "single-layer trunk"
    assert seq % ATT_BLOCK == 0 and peer_sub_keys.shape[1:] == (PEER_HEADS, 2, PEER_NKEYS, PEER_DHALF)
    t = batch * seq
    x2d = x.reshape(t, d)

    wi = w_in[0]
    o_q, o_k, o_v, o_hq = 0, Q_COLS, Q_COLS + KV_COLS, Q_COLS + 2 * KV_COLS
    w_perm = jnp.concatenate([wi[:, o_q:o_k], wi[:, o_hq:], wi[:, o_k:o_hq]], axis=1).astype(BF16)
    col_q, col_hq, col_k, col_v = 0, Q_COLS, Q_COLS + 4 * HK, Q_COLS + 4 * HK + KV_COLS

    proj = _in_proj(x2d, ln_mix_g[0], w_perm, _tile(t, 1024), 512)
    proj_meta = _in_proj(meta_tokens.astype(F32), ln_mix_g[0], w_perm, N_META, 512)

    att = _attention(proj, proj_meta, attn_sinks[0].astype(F32), q_norm_g[0], k_norm_g[0],
                     batch, seq, col_q, col_k, col_v)
    rec = _hgrn(proj, proj_meta, hgrn_lb_logits.astype(F32), hgrn_norm_g[0], batch, seq, col_hq,
                _tile(seq, 256))

    wo = w_out[0].astype(BF16)
    h2, xnt = _out_proj(x2d, att, rec, wo[:Q_COLS], wo[Q_COLS:], ln_ffn_g[0], _tile(t, 256))

    wqt = jnp.transpose(peer_w_q[0]).astype(BF16)
    keys = peer_sub_keys[0].astype(BF16)
    a1, a2, tau = _peer_keys(xnt, wqt, keys, _tile(t, 512))

    u = peer_u[0].astype(BF16)
    vt = jnp.transpose(peer_v[0]).astype(BF16)
    peer_t = _peer_dense(xnt, u, vt, a1, a2, tau, _tile(t, 1024), 512)

    out = _finish(h2, peer_t, _tile(t, 256))
    return out.reshape(batch, seq, d)
```

```python
import functools
import math

import jax
import jax.numpy as jnp
import numpy as np
from jax import lax
from jax.experimental import pallas as pl
from jax.experimental.pallas import tpu as pltpu

F32 = jnp.float32
BF16 = jnp.bfloat16

N_META = 16
RMS_EPS = 1e-6
HEAD_DIM = 128
ATT_Q_HEADS = 8
ATT_KV_HEADS = 2
ATT_GROUP = ATT_Q_HEADS // ATT_KV_HEADS
ATT_BLOCK = 128
ROPE_THETA = 500000.0
ROPE_DIM = HEAD_DIM // 4
HGRN_HEADS = 8
HGRN_DK = 128
HGRN_CHUNK = 64
Q_COLS = ATT_Q_HEADS * HEAD_DIM
KV_COLS = ATT_KV_HEADS * HEAD_DIM
HK = HGRN_HEADS * HGRN_DK
PEER_HEADS = 8
PEER_NKEYS = 128
PEER_DHALF = 128
PEER_TOPK = 16

V7X_VMEM_BYTES = 64 * 1024 * 1024
VMEM_LIMIT = V7X_VMEM_BYTES - 8 * 1024 * 1024
LANES = 128
HGRN_EXP_CLAMP = 85.0
MASKED = -1e30


def _params(*sem):
    return pltpu.CompilerParams(dimension_semantics=sem, vmem_limit_bytes=VMEM_LIMIT)


def _rms(x, g):
    ms = jnp.mean(x * x, axis=-1, keepdims=True)
    return x * lax.rsqrt(ms + RMS_EPS) * g


def _sigmoid(x):
    return 1.0 / (1.0 + jnp.exp(-x))


def _in_proj_kernel(x_ref, g_ref, w_ref, o_ref, xn_ref):
    @pl.when(pl.program_id(1) == 0)
    def _():
        xn_ref[...] = _rms(x_ref[...], g_ref[...]).astype(BF16)

    o_ref[...] = jnp.dot(xn_ref[...], w_ref[...], preferred_element_type=F32)


def _in_proj(x2d, g, w, tm, tn):
    m, d = x2d.shape
    n = w.shape[1]
    return pl.pallas_call(
        _in_proj_kernel,
        grid=(m // tm, n // tn),
        in_specs=[
            pl.BlockSpec((tm, d), lambda i, j: (i, 0)),
            pl.BlockSpec((1, d), lambda i, j: (0, 0)),
            pl.BlockSpec((d, tn), lambda i, j: (0, j)),
        ],
        out_specs=pl.BlockSpec((tm, tn), lambda i, j: (i, j)),
        out_shape=jax.ShapeDtypeStruct((m, n), F32),
        scratch_shapes=[pltpu.VMEM((tm, d), BF16)],
        compiler_params=_params("arbitrary", "arbitrary"),
        name="in_proj",
    )(x2d, g.reshape(1, d), w)


def _rope(x, tab_ref):
    return (x * tab_ref[0]
            + pltpu.roll(x, HEAD_DIM - ROPE_DIM // 2, 1) * tab_ref[1]
            + pltpu.roll(x, ROPE_DIM // 2, 1) * tab_ref[2])


def _attn_kernel(sink_ref, q_ref, kp_ref, ko_ref, vp_ref, vo_ref, km_ref, vm_ref,
                 qg_ref, kg_ref, tab_ref, tabp_ref, tabm_ref, o_ref):
    n = pl.program_id(1)
    blk = ATT_BLOCK
    rows = ATT_GROUP * blk
    scale = HEAD_DIM ** -0.5
    qi = lax.broadcasted_iota(jnp.int32, (rows, blk), 0) & (blk - 1)
    ki = lax.broadcasted_iota(jnp.int32, (rows, blk), 1)
    own_mask = ki <= qi
    prev_mask = ki > jnp.where(n > 0, qi, blk)
    head_row = lax.shift_right_logical(lax.broadcasted_iota(jnp.int32, (rows, 1), 0), 7)
    nt = (((1,), (1,)), ((), ()))
    for g in range(ATT_KV_HEADS):
        ksl = slice(g * HEAD_DIM, (g + 1) * HEAD_DIM)
        k_own = _rope(_rms(ko_ref[:, ksl], kg_ref[...]), tab_ref).astype(BF16)
        k_prev = _rope(_rms(kp_ref[:, ksl], kg_ref[...]), tabp_ref).astype(BF16)
        k_meta = _rope(_rms(km_ref[:, ksl], kg_ref[...]), tabm_ref).astype(BF16)
        qs = []
        sink = jnp.zeros((rows, 1), F32)
        for hh in range(ATT_GROUP):
            h = g * ATT_GROUP + hh
            qh = q_ref[:, h * HEAD_DIM:(h + 1) * HEAD_DIM]
            qs.append(_rope(_rms(qh, qg_ref[...]), tab_ref))
            sink = jnp.where(head_row == hh, sink_ref[h], sink)
        q = jnp.concatenate(qs, axis=0).astype(BF16)
        s_own = lax.dot_general(q, k_own, nt, preferred_element_type=F32) * scale
        s_prev = lax.dot_general(q, k_prev, nt, preferred_element_type=F32) * scale
        s_meta = lax.dot_general(q, k_meta, nt, preferred_element_type=F32) * scale
        s_own = jnp.where(own_mask, s_own, MASKED)
        s_prev = jnp.where(prev_mask, s_prev, MASKED)
        m = jnp.maximum(jnp.maximum(jnp.max(s_own, axis=-1, keepdims=True),
                                    jnp.max(s_prev, axis=-1, keepdims=True)),
                        jnp.maximum(jnp.max(s_meta, axis=-1, keepdims=True), sink))
        p_own = jnp.exp(s_own - m)
        p_prev = jnp.exp(s_prev - m)
        p_meta = jnp.exp(s_meta - m)
        denom = (jnp.sum(p_own, axis=-1, keepdims=True) + jnp.sum(p_prev, axis=-1, keepdims=True)
                 + jnp.sum(p_meta, axis=-1, keepdims=True) + jnp.exp(sink - m))
        acc = (jnp.dot(p_own.astype(BF16), vo_ref[:, ksl].astype(BF16), preferred_element_type=F32)
               + jnp.dot(p_prev.astype(BF16), vp_ref[:, ksl].astype(BF16), preferred_element_type=F32)
               + jnp.dot(p_meta.astype(BF16), vm_ref[:, ksl].astype(BF16), preferred_element_type=F32))
        out = acc / denom
        for hh in range(ATT_GROUP):
            h = g * ATT_GROUP + hh
            o_ref[:, h * HEAD_DIM:(h + 1) * HEAD_DIM] = out[hh * blk:(hh + 1) * blk].astype(BF16)


def _rope_tables(pos):
    half = ROPE_DIM // 2
    inv = ROPE_THETA ** (-jnp.arange(0, ROPE_DIM, 2, dtype=F32) / ROPE_DIM)
    ang = pos.astype(F32)[:, None] * inv[None, :]
    cos, sin = jnp.cos(ang), jnp.sin(ang)
    n = pos.shape[0]
    ones = jnp.ones((n, HEAD_DIM - ROPE_DIM), F32)
    zeros = jnp.zeros((n, HEAD_DIM - half), F32)
    c = jnp.concatenate([cos, cos, ones], axis=1)
    s1 = jnp.concatenate([-sin, zeros], axis=1)
    s2 = jnp.concatenate([jnp.zeros((n, half), F32), sin, jnp.zeros((n, HEAD_DIM - ROPE_DIM), F32)], axis=1)
    return jnp.stack([c, s1, s2])


def _attention(proj, proj_meta, sinks, qg, kg, batch, seq, col_q, col_k, col_v):
    nb = seq // ATT_BLOCK
    t = batch * seq
    tab = _rope_tables(jnp.arange(seq) + N_META)
    tabm = _rope_tables(jnp.arange(N_META))
    kb, vb = col_k // KV_COLS, col_v // KV_COLS

    def own(b, n):
        return b * nb + n

    def prev(b, n):
        return b * nb + jnp.maximum(n - 1, 0)

    return pl.pallas_call(
        _attn_kernel,
        grid=(batch, nb),
        in_specs=[
            pl.BlockSpec(memory_space=pltpu.SMEM),
            pl.BlockSpec((ATT_BLOCK, Q_COLS), lambda b, n: (own(b, n), col_q // Q_COLS)),
            pl.BlockSpec((ATT_BLOCK, KV_COLS), lambda b, n: (prev(b, n), kb)),
            pl.BlockSpec((ATT_BLOCK, KV_COLS), lambda b, n: (own(b, n), kb)),
            pl.BlockSpec((ATT_BLOCK, KV_COLS), lambda b, n: (prev(b, n), vb)),
            pl.BlockSpec((ATT_BLOCK, KV_COLS), lambda b, n: (own(b, n), vb)),
            pl.BlockSpec((N_META, KV_COLS), lambda b, n: (0, kb)),
            pl.BlockSpec((N_META, KV_COLS), lambda b, n: (0, vb)),
            pl.BlockSpec((1, HEAD_DIM), lambda b, n: (0, 0)),
            pl.BlockSpec((1, HEAD_DIM), lambda b, n: (0, 0)),
            pl.BlockSpec((3, ATT_BLOCK, HEAD_DIM), lambda b, n: (0, n, 0)),
            pl.BlockSpec((3, ATT_BLOCK, HEAD_DIM), lambda b, n: (0, jnp.maximum(n - 1, 0), 0)),
            pl.BlockSpec((3, N_META, HEAD_DIM), lambda b, n: (0, 0, 0)),
        ],
        out_specs=pl.BlockSpec((ATT_BLOCK, Q_COLS), lambda b, n: (own(b, n), 0)),
        out_shape=jax.ShapeDtypeStruct((t, Q_COLS), BF16),
        compiler_params=_params("arbitrary", "arbitrary"),
        name="swa_attention",
    )(sinks, proj, proj, proj, proj, proj, proj_meta, proj_meta,
      qg.reshape(1, HEAD_DIM), kg.reshape(1, HEAD_DIM), tab, tab, tabm)


def _cumsum_rows(x):
    n = x.shape[0]
    row = lax.broadcasted_iota(jnp.int32, x.shape, 0)
    shift = 1
    while shift < n:
        x = x + jnp.where(row >= shift, pltpu.roll(x, shift, 0), 0.0)
        shift *= 2
    return x


def _hgrn_chunk(hq, hf, hi, hg, lb, ng, st_ref, valid):
    c = hq.shape[0]
    f = lb + (1.0 - lb) * _sigmoid(hf)
    logf = jnp.log(f)
    kk = (1.0 - lb) * _sigmoid(-hf)
    if valid is not None:
        logf = jnp.where(valid, logf, 0.0)
        kk = jnp.where(valid, kk, 0.0)
    q = hq * _sigmoid(hq)
    b = _cumsum_rows(logf)
    b_last = b[c - 1:c, :]
    b_mid = b[c // 2 - 1:c // 2, :]
    q_mid = (q * jnp.exp(jnp.minimum(b - b_mid, HGRN_EXP_CLAMP))).astype(BF16)
    k_mid = (kk * jnp.exp(jnp.minimum(b_mid - b, HGRN_EXP_CLAMP))).astype(BF16)
    q_in = (q * jnp.exp(b)).astype(BF16)
    k_out = (kk * jnp.exp(b_last - b)).astype(BF16)
    decay = jnp.exp(b_last)
    v = hi.astype(BF16)
    causal = (lax.broadcasted_iota(jnp.int32, (c, c), 0) >= lax.broadcasted_iota(jnp.int32, (c, c), 1))
    nt = (((1,), (1,)), ((), ()))
    tn = (((0,), (0,)), ((), ()))
    outs = []
    for h in range(HGRN_HEADS):
        sl = slice(h * HGRN_DK, (h + 1) * HGRN_DK)
        st = st_ref[h]
        scores = lax.dot_general(q_mid[:, sl], k_mid[:, sl], nt, preferred_element_type=F32)
        scores = jnp.where(causal, scores, 0.0).astype(BF16)
        o = (jnp.dot(scores, v[:, sl], preferred_element_type=F32)
             + lax.dot_general(q_in[:, sl], st.astype(BF16), nt, preferred_element_type=F32))
        st_ref[h] = (st * decay[:, sl]
                     + lax.dot_general(v[:, sl], k_out[:, sl], tn, preferred_element_type=F32))
        outs.append(_rms(o, ng[:, sl]))
    o_all = jnp.concatenate(outs, axis=1)
    return o_all * (hg * _sigmoid(hg))


def _lower_bound(lbl_ref):
    l0, l1 = lbl_ref[0:1, :], lbl_ref[1:2, :]
    m = jnp.maximum(l0, l1)
    e0, e1 = jnp.exp(l0 - m), jnp.exp(l1 - m)
    return e0 / (e0 + e1)


def _hgrn_meta_kernel(hq_ref, hf_ref, hi_ref, hg_ref, lbl_ref, ng_ref, st_out_ref):
    st_out_ref[...] = jnp.zeros_like(st_out_ref)
    _hgrn_chunk(hq_ref[...], hf_ref[...], hi_ref[...], hg_ref[...], _lower_bound(lbl_ref), ng_ref[...],
                st_out_ref, None)


def _hgrn_kernel(hq_ref, hf_ref, hi_ref, hg_ref, lbl_ref, ng_ref, st0_ref, o_ref, st_ref, *, chunks):
    @pl.when(pl.program_id(1) == 0)
    def _():
        st_ref[...] = st0_ref[...]

    lb = _lower_bound(lbl_ref)

    def body(ci, carry):
        rows = pl.ds(pl.multiple_of(ci * HGRN_CHUNK, HGRN_CHUNK), HGRN_CHUNK)
        o = _hgrn_chunk(hq_ref[rows, :], hf_ref[rows, :], hi_ref[rows, :], hg_ref[rows, :], lb, ng_ref[...],
                        st_ref, None)
        o_ref[rows, :] = o.astype(BF16)
        return carry

    lax.fori_loop(0, chunks, body, 0)


def _hgrn(proj, proj_meta, lb_logits, norm_g, batch, seq, col_hq, rows_per_step):
    t = batch * seq
    cb = col_hq // HK
    state_shape = (HGRN_HEADS, HGRN_DK, HGRN_DK)
    vec = pl.BlockSpec((1, HK), lambda *_: (0, 0))
    lbs = pl.BlockSpec((2, HK), lambda *_: (0, 0))
    st_meta = pl.pallas_call(
        _hgrn_meta_kernel,
        grid=(1,),
        in_specs=[pl.BlockSpec((N_META, HK), lambda i, k=k: (0, cb + k)) for k in range(4)] + [lbs, vec],
        out_specs=pl.BlockSpec(state_shape, lambda i: (0, 0, 0)),
        out_shape=jax.ShapeDtypeStruct(state_shape, F32),
        compiler_params=_params("arbitrary"),
        name="hgrn2_meta",
    )(proj_meta, proj_meta, proj_meta, proj_meta, lb_logits, norm_g.reshape(1, HK))
    steps = seq // rows_per_step
    return pl.pallas_call(
        functools.partial(_hgrn_kernel, chunks=rows_per_step // HGRN_CHUNK),
        grid=(batch, steps),
        in_specs=[pl.BlockSpec((rows_per_step, HK), lambda b, s, k=k: (b * steps + s, cb + k)) for k in range(4)]
        + [lbs, vec, pl.BlockSpec(state_shape, lambda b, s: (0, 0, 0))],
        out_specs=pl.BlockSpec((rows_per_step, HK), lambda b, s: (b * steps + s, 0)),
        out_shape=jax.ShapeDtypeStruct((t, HK), BF16),
        scratch_shapes=[pltpu.VMEM(state_shape, F32)],
        compiler_params=_params("arbitrary", "arbitrary"),
        name="hgrn2",
    )(proj, proj, proj, proj, lb_logits, norm_g.reshape(1, HK), st_meta)


def _out_proj_kernel(x_ref, att_ref, rec_ref, wa_ref, wr_ref, g_ref, h_ref, xnt_ref):
    h = (x_ref[...]
         + jnp.dot(att_ref[...], wa_ref[...], preferred_element_type=F32)
         + jnp.dot(rec_ref[...], wr_ref[...], preferred_element_type=F32))
    h_ref[...] = h
    xnt_ref[...] = jnp.transpose(_rms(h, g_ref[...])).astype(BF16)


def _out_proj(x2d, att, rec, wa, wr, g, tm):
    t, d = x2d.shape
    return pl.pallas_call(
        _out_proj_kernel,
        grid=(t // tm,),
        in_specs=[
            pl.BlockSpec((tm, d), lambda i: (i, 0)),
            pl.BlockSpec((tm, att.shape[1]), lambda i: (i, 0)),
            pl.BlockSpec((tm, rec.shape[1]), lambda i: (i, 0)),
            pl.BlockSpec(wa.shape, lambda i: (0, 0)),
            pl.BlockSpec(wr.shape, lambda i: (0, 0)),
            pl.BlockSpec((1, d), lambda i: (0, 0)),
        ],
        out_specs=[pl.BlockSpec((tm, d), lambda i: (i, 0)), pl.BlockSpec((d, tm), lambda i: (0, i))],
        out_shape=[jax.ShapeDtypeStruct((t, d), F32), jax.ShapeDtypeStruct((d, t), BF16)],
        compiler_params=_params("arbitrary"),
        name="out_proj",
    )(x2d, att, rec, wa, wr, g.reshape(1, d))


def _sort_desc(v):
    v = list(v)
    n = len(v)
    k = 2
    while k <= n:
        j = k // 2
        while j >= 1:
            for i in range(n):
                l = i ^ j
                if l > i:
                    hi, lo = jnp.maximum(v[i], v[l]), jnp.minimum(v[i], v[l])
                    v[i], v[l] = (hi, lo) if (i & k) == 0 else (lo, hi)
            j //= 2
        k *= 2
    return v


def _merge_desc(v):
    v = list(v)
    j = len(v) // 2
    while j >= 1:
        for i in range(len(v)):
            l = i ^ j
            if l > i:
                v[i], v[l] = jnp.maximum(v[i], v[l]), jnp.minimum(v[i], v[l])
        j //= 2
    return v


def _top_merge(a, b):
    n = len(a)
    return _merge_desc([jnp.maximum(a[i], b[n - 1 - i]) for i in range(n)])


def _top_products(w1, w2):
    k = PEER_TOPK
    rows = [[w1[i] * w2[j] for j in range(k // (i + 1))] for i in range(k // 2)]
    tail = [w1[i] * w2[0] for i in range(k // 2, k)]
    b1 = _merge_desc(rows[1] + tail[::-1])
    b2 = _sort_desc(rows[2] + rows[3] + rows[4] + rows[5] + rows[6])
    t = _top_merge(_top_merge(rows[0], b1), b2)
    t[k - 1] = jnp.maximum(t[k - 1], rows[7][0])
    t[k - 2] = jnp.maximum(t[k - 2], rows[7][1])
    return _merge_desc(t)


def _top16_of_rows(s_ref, c, groups):
    blocks = []
    for b in range(PEER_NKEYS // PEER_TOPK):
        blocks.append(_sort_desc([s_ref[c, (b * PEER_TOPK + k) * 8:(b * PEER_TOPK + k + 1) * 8, :]
                                  for k in range(PEER_TOPK)]))
    while len(blocks) > 1:
        blocks = [_top_merge(blocks[i], blocks[i + 1]) for i in range(0, len(blocks), 2)]
    return blocks[0]


def _peer_keys_kernel(xt_ref, wqt_ref, keys_ref, a1_ref, a2_ref, tau_ref, s_s):
    tm = xt_ref.shape[1]
    groups = tm // LANES
    if groups < 8:
        s_s[...] = jnp.zeros_like(s_s)

    def body(h, carry):
        r0 = pl.multiple_of(h * 2 * PEER_DHALF, 2 * PEER_DHALF)
        q = jnp.dot(wqt_ref[pl.ds(r0, 2 * PEER_DHALF), :], xt_ref[...], preferred_element_type=F32).astype(BF16)
        s1 = jnp.dot(keys_ref[h, 0], q[:PEER_DHALF], preferred_element_type=F32)
        s2 = jnp.dot(keys_ref[h, 1], q[PEER_DHALF:], preferred_element_type=F32)
        a2_ref[h] = jnp.exp(s2 - jnp.max(s2, axis=0, keepdims=True))
        for g in range(groups):
            s_s[0, pl.ds(g, PEER_NKEYS, stride=8), :] = s1[:, g * LANES:(g + 1) * LANES]
            s_s[1, pl.ds(g, PEER_NKEYS, stride=8), :] = s2[:, g * LANES:(g + 1) * LANES]
        t1 = _top16_of_rows(s_s, 0, groups)
        t2 = _top16_of_rows(s_s, 1, groups)
        w1 = [jnp.exp(t - t1[0]) for t in t1]
        w2 = [jnp.exp(t - t2[0]) for t in t2]
        top = _top_products(w1, w2)
        z = top[0]
        for m in top[1:]:
            z = z + m
        inv_z = 1.0 / z
        tau_ref[h] = _top_products([w * inv_z for w in w1], w2)[PEER_TOPK - 1][:groups]
        for k in range(PEER_NKEYS):
            a1_ref[h, k] = (jnp.exp(s_s[0, k * 8:(k + 1) * 8, :] - t1[0]) * inv_z)[:groups]
        return carry

    lax.fori_loop(0, PEER_HEADS, body, 0)


def _peer_keys(xnt, wqt, keys, tm):
    d, t = xnt.shape
    dq = wqt.shape[0]
    groups = tm // LANES
    return pl.pallas_call(
        _peer_keys_kernel,
        grid=(t // tm,),
        in_specs=[
            pl.BlockSpec((d, tm), lambda i: (0, i)),
            pl.BlockSpec((dq, d), lambda i: (0, 0)),
            pl.BlockSpec(keys.shape, lambda i: (0, 0, 0, 0)),
        ],
        out_specs=[
            pl.BlockSpec((PEER_HEADS, PEER_NKEYS, groups, LANES), lambda i: (0, 0, i, 0)),
            pl.BlockSpec((PEER_HEADS, PEER_NKEYS, tm), lambda i: (0, 0, i)),
            pl.BlockSpec((PEER_HEADS, groups, LANES), lambda i: (0, i, 0)),
        ],
        out_shape=[
            jax.ShapeDtypeStruct((PEER_HEADS, PEER_NKEYS, t // LANES, LANES), F32),
            jax.ShapeDtypeStruct((PEER_HEADS, PEER_NKEYS, t), F32),
            jax.ShapeDtypeStruct((PEER_HEADS, t // LANES, LANES), F32),
        ],
        scratch_shapes=[pltpu.VMEM((2, PEER_NKEYS * 8, LANES), F32)],
        compiler_params=_params("arbitrary"),
        name="peer_keys",
    )(xnt, wqt, keys)


def _gelu_tanh(x):
    c = math.sqrt(2.0 / math.pi)
    return x * (0.5 * (1.0 + jnp.tanh(c * (x + 0.044715 * (x * x * x)))))


def _peer_dense_kernel(xt_ref, u_ref, vt_ref, a1_ref, a2_ref, tau_ref, o_ref, a_even, a_odd, *, chunk):
    e = pl.program_id(1)
    eb, tm = a_even.shape
    keys_per_step = eb // PEER_NKEYS
    key_base = jnp.maximum(e - 1, 0) * keys_per_step

    @pl.when(e == 0)
    def _():
        o_ref[...] = jnp.zeros_like(o_ref)
        a_odd[...] = jnp.zeros_like(a_odd)

    def run(a_write, a_read):
        for q in range(tm // chunk):
            lanes = slice(q * chunk, (q + 1) * chunk)
            a_write[:, lanes] = jnp.dot(u_ref[...], xt_ref[:, lanes], preferred_element_type=F32)
            tiles = []
            for k in range(keys_per_step):
                rows = slice(k * PEER_NKEYS, (k + 1) * PEER_NKEYS)
                cols = []
                for g in range(q * chunk // LANES, (q + 1) * chunk // LANES):
                    glanes = slice(g * LANES, (g + 1) * LANES)
                    gate = jnp.zeros((PEER_NKEYS, LANES), F32)
                    for h in range(PEER_HEADS):
                        p = a1_ref[h, pl.ds(key_base + k, 1), g, :] * a2_ref[h, :, glanes]
                        gate = gate + jnp.where(p >= tau_ref[h, g:g + 1, :], p, 0.0)
                    cols.append((_gelu_tanh(a_read[rows, glanes]) * gate).astype(BF16))
                tiles.append(jnp.concatenate(cols, axis=1))
            half = len(tiles) // 2
            h_lo = jnp.concatenate(tiles[:half], axis=0)
            h_hi = jnp.concatenate(tiles[half:], axis=0)
            o_ref[:, lanes] += (jnp.dot(vt_ref[:, :eb // 2], h_lo, preferred_element_type=F32)
                                + jnp.dot(vt_ref[:, eb // 2:], h_hi, preferred_element_type=F32))

    @pl.when(e % 2 == 0)
    def _():
        run(a_even, a_odd)

    @pl.when(e % 2 == 1)
    def _():
        run(a_odd, a_even)


def _peer_dense(xnt, u, vt, a1, a2, tau, tm, eb):
    d, t = xnt.shape
    n_e = u.shape[0] // eb
    groups = tm // LANES
    return pl.pallas_call(
        functools.partial(_peer_dense_kernel, chunk=min(tm, 256)),
        grid=(t // tm, n_e + 1),
        in_specs=[
            pl.BlockSpec((d, tm), lambda i, e: (0, i)),
            pl.BlockSpec((eb, d), lambda i, e: (jnp.minimum(e, n_e - 1), 0)),
            pl.BlockSpec((d, eb), lambda i, e: (0, jnp.maximum(e - 1, 0))),
            pl.BlockSpec((PEER_HEADS, PEER_NKEYS, groups, LANES), lambda i, e: (0, 0, i, 0)),
            pl.BlockSpec((PEER_HEADS, PEER_NKEYS, tm), lambda i, e: (0, 0, i)),
            pl.BlockSpec((PEER_HEADS, groups, LANES), lambda i, e: (0, i, 0)),
        ],
        out_specs=pl.BlockSpec((d, tm), lambda i, e: (0, i)),
        out_shape=jax.ShapeDtypeStruct((d, t), F32),
        scratch_shapes=[pltpu.VMEM((eb, tm), F32), pltpu.VMEM((eb, tm), F32)],
        compiler_params=_params("arbitrary", "arbitrary"),
        name="peer_dense",
    )(xnt, u, vt, a1, a2, tau)


def _finish_kernel(h_ref, pt_ref, o_ref):
    o_ref[...] = h_ref[...] + jnp.transpose(pt_ref[...])


def _finish(h2, peer_t, tm):
    t, d = h2.shape
    return pl.pallas_call(
        _finish_kernel,
        grid=(t // tm,),
        in_specs=[pl.BlockSpec((tm, d), lambda i: (i, 0)), pl.BlockSpec((d, tm), lambda i: (0, i))],
        out_specs=pl.BlockSpec((tm, d), lambda i: (i, 0)),
        out_shape=jax.ShapeDtypeStruct((t, d), F32),
        compiler_params=_params("arbitrary"),
        name="finish",
    )(h2, peer_t)


def _tile(n, want):
    t = min(n, want)
    assert n % t == 0, (n, want)
    return t


def kernel(x, meta_tokens, hgrn_lb_logits, ln_mix_g, w_in, q_norm_g, k_norm_g, attn_sinks, hgrn_norm_g,
           w_out, ln_ffn_g, peer_w_q, peer_sub_keys, peer_u, peer_v):
    batch, seq, d = x.shape
    assert w_in.shape[0] == 1 and hgrn_lb_logits.shape[0] == 2, "single-layer trunk"
    assert seq % ATT_BLOCK == 0 and peer_sub_keys.shape[1:] == (PEER_HEADS, 2, PEER_NKEYS, PEER_DHALF)
    t = batch * seq
    x2d = x.reshape(t, d)

    wi = w_in[0]
    o_q, o_k, o_v, o_hq = 0, Q_COLS, Q_COLS + KV_COLS, Q_COLS + 2 * KV_COLS
    w_perm = jnp.concatenate([wi[:, o_q:o_k], wi[:, o_hq:], wi[:, o_k:o_hq]], axis=1).astype(BF16)
    col_q, col_hq, col_k, col_v = 0, Q_COLS, Q_COLS + 4 * HK, Q_COLS + 4 * HK + KV_COLS

    proj = _in_proj(x2d, ln_mix_g[0], w_perm, _tile(t, 1024), 512)
    proj_meta = _in_proj(meta_tokens.astype(F32), ln_mix_g[0], w_perm, N_META, 512)

    att = _attention(proj, proj_meta, attn_sinks[0].astype(F32), q_norm_g[0], k_norm_g[0],
                     batch, seq, col_q, col_k, col_v)
    rec = _hgrn(proj, proj_meta, hgrn_lb_logits.astype(F32), hgrn_norm_g[0], batch, seq, col_hq,
                _tile(seq, 256))

    wo = w_out[0].astype(BF16)
    h2, xnt = _out_proj(x2d, att, rec, wo[:Q_COLS], wo[Q_COLS:], ln_ffn_g[0], _tile(t, 256))

    wqt = jnp.transpose(peer_w_q[0]).astype(BF16)
    keys = peer_sub_keys[0].astype(BF16)
    a1, a2, tau = _peer_keys(xnt, wqt, keys, _tile(t, 1024))

    u = peer_u[0].astype(BF16)
    vt = jnp.transpose(peer_v[0]).astype(BF16)
    peer_t = _peer_dense(xnt, u, vt, a1, a2, tau, _tile(t, 1024), 512)

    out = _finish(h2, peer_t, _tile(t, 256))
    return out.reshape(batch, seq, d)
```

```python
import functools
import math

import jax
import jax.numpy as jnp
import numpy as np
from jax import lax
from jax.experimental import pallas as pl
from jax.experimental.pallas import tpu as pltpu

F32 = jnp.float32
BF16 = jnp.bfloat16

N_META = 16
RMS_EPS = 1e-6
HEAD_DIM = 128
ATT_Q_HEADS = 8
ATT_KV_HEADS = 2
ATT_GROUP = ATT_Q_HEADS // ATT_KV_HEADS
ATT_BLOCK = 128
ROPE_THETA = 500000.0
ROPE_DIM = HEAD_DIM // 4
HGRN_HEADS = 8
HGRN_DK = 128
HGRN_CHUNK = 64
Q_COLS = ATT_Q_HEADS * HEAD_DIM
KV_COLS = ATT_KV_HEADS * HEAD_DIM
HK = HGRN_HEADS * HGRN_DK
IN_COLS = Q_COLS + 2 * KV_COLS + 4 * HK
IN_PROJ_TN = IN_COLS // 4
PEER_HEADS = 8
PEER_NKEYS = 128
PEER_DHALF = 128
PEER_TOPK = 16

V7X_VMEM_BYTES = 64 * 1024 * 1024
VMEM_LIMIT = V7X_VMEM_BYTES - 8 * 1024 * 1024
LANES = 128
HGRN_EXP_CLAMP = 85.0
MASKED = -1e30


def _params(*sem):
    return pltpu.CompilerParams(dimension_semantics=sem, vmem_limit_bytes=VMEM_LIMIT)


def _rms(x, g):
    ms = jnp.mean(x * x, axis=-1, keepdims=True)
    return x * lax.rsqrt(ms + RMS_EPS) * g


def _sigmoid(x):
    return 1.0 / (1.0 + jnp.exp(-x))


def _in_proj_kernel(x_ref, g_ref, w_ref, o_ref, xn_ref):
    @pl.when(pl.program_id(1) == 0)
    def _():
        xn_ref[...] = _rms(x_ref[...], g_ref[...]).astype(BF16)

    o_ref[...] = jnp.dot(xn_ref[...], w_ref[...], preferred_element_type=F32)


def _in_proj(x2d, g, w, tm, tn):
    m, d = x2d.shape
    n = w.shape[1]
    return pl.pallas_call(
        _in_proj_kernel,
        grid=(m // tm, n // tn),
        in_specs=[
            pl.BlockSpec((tm, d), lambda i, j: (i, 0)),
            pl.BlockSpec((1, d), lambda i, j: (0, 0)),
            pl.BlockSpec((d, tn), lambda i, j: (0, j)),
        ],
        out_specs=pl.BlockSpec((tm, tn), lambda i, j: (i, j)),
        out_shape=jax.ShapeDtypeStruct((m, n), F32),
        scratch_shapes=[pltpu.VMEM((tm, d), BF16)],
        compiler_params=_params("arbitrary", "arbitrary"),
        name="in_proj",
    )(x2d, g.reshape(1, d), w)


ROPE_HALF = ROPE_DIM // 2
ROPE_LANE_ORDER = np.concatenate([
    np.arange(0, ROPE_HALF), np.arange(ROPE_DIM, ROPE_DIM + HEAD_DIM // 2 - ROPE_HALF),
    np.arange(ROPE_HALF, ROPE_DIM), np.arange(ROPE_DIM + HEAD_DIM // 2 - ROPE_HALF, HEAD_DIM)])


def _rope(x, tab_ref):
    return x * tab_ref[0] + pltpu.roll(x, HEAD_DIM // 2, 1) * tab_ref[1]


def _attn_kernel(sink_ref, q_ref, k_ref, v_ref, km_ref, vm_ref, qg_ref, kg_ref, tab_ref, tabm_ref, o_ref,
                 kprev_s, vprev_s, kmeta_s, vmeta_s):
    b, n = pl.program_id(0), pl.program_id(1)
    blk = ATT_BLOCK
    rows = ATT_GROUP * blk
    scale = HEAD_DIM ** -0.5

    @pl.when(jnp.logical_and(b == 0, n == 0))
    def _():
        pad = jnp.zeros((blk - N_META, HEAD_DIM), BF16)
        for g in range(ATT_KV_HEADS):
            ksl = slice(g * HEAD_DIM, (g + 1) * HEAD_DIM)
            k_meta = _rope(_rms(km_ref[:, ksl], kg_ref[...]), tabm_ref).astype(BF16)
            kmeta_s[g] = jnp.concatenate([k_meta, pad], axis=0)
            vmeta_s[g] = jnp.concatenate([vm_ref[:, ksl].astype(BF16), pad], axis=0)

    @pl.when(n == 0)
    def _():
        kprev_s[...] = jnp.zeros_like(kprev_s)
        vprev_s[...] = jnp.zeros_like(vprev_s)

    qi = lax.broadcasted_iota(jnp.int32, (rows, blk), 0) & (blk - 1)
    ki = lax.broadcasted_iota(jnp.int32, (rows, blk), 1)
    own_mask = ki <= qi
    prev_mask = ki > jnp.where(n > 0, qi, blk)
    meta_mask = ki < N_META
    head_row = lax.shift_right_logical(lax.broadcasted_iota(jnp.int32, (rows, 1), 0), 7)
    nt = (((1,), (1,)), ((), ()))
    for g in range(ATT_KV_HEADS):
        ksl = slice(g * HEAD_DIM, (g + 1) * HEAD_DIM)
        k_own = _rope(_rms(k_ref[:, ksl], kg_ref[...]), tab_ref).astype(BF16)
        v_own = v_ref[:, ksl].astype(BF16)
        keys = jnp.concatenate([kprev_s[g], k_own, kmeta_s[g]], axis=0)
        vals = jnp.concatenate([vprev_s[g], v_own, vmeta_s[g]], axis=0)
        kprev_s[g] = k_own
        vprev_s[g] = v_own
        qs = []
        sink = jnp.zeros((rows, 1), F32)
        for hh in range(ATT_GROUP):
            h = g * ATT_GROUP + hh
            qs.append(_rope(_rms(q_ref[:, h * HEAD_DIM:(h + 1) * HEAD_DIM], qg_ref[...]), tab_ref))
            sink = jnp.where(head_row == hh, sink_ref[h], sink)
        q = jnp.concatenate(qs, axis=0).astype(BF16)
        s = lax.dot_general(q, keys, nt, preferred_element_type=F32) * scale
        s_prev = jnp.where(prev_mask, s[:, :blk], MASKED)
        s_own = jnp.where(own_mask, s[:, blk:2 * blk], MASKED)
        s_meta = jnp.where(meta_mask, s[:, 2 * blk:], MASKED)
        m = jnp.maximum(jnp.max(jnp.maximum(jnp.maximum(s_prev, s_own), s_meta), axis=-1, keepdims=True), sink)
        p_prev, p_own, p_meta = jnp.exp(s_prev - m), jnp.exp(s_own - m), jnp.exp(s_meta - m)
        denom = jnp.sum(p_prev + p_own + p_meta, axis=-1, keepdims=True) + jnp.exp(sink - m)
        p = jnp.concatenate([p_prev, p_own, p_meta], axis=1).astype(BF16)
        out = jnp.dot(p, vals, preferred_element_type=F32) / denom
        for hh in range(ATT_GROUP):
            h = g * ATT_GROUP + hh
            o_ref[:, h * HEAD_DIM:(h + 1) * HEAD_DIM] = out[hh * blk:(hh + 1) * blk].astype(BF16)


def _rope_tables(pos):
    inv = ROPE_THETA ** (-jnp.arange(0, ROPE_DIM, 2, dtype=F32) / ROPE_DIM)
    ang = pos.astype(F32)[:, None] * inv[None, :]
    cos, sin = jnp.cos(ang), jnp.sin(ang)
    n = pos.shape[0]
    gap = HEAD_DIM // 2 - ROPE_HALF
    c = jnp.concatenate([cos, jnp.ones((n, gap), F32), cos, jnp.ones((n, gap), F32)], axis=1)
    s = jnp.concatenate([-sin, jnp.zeros((n, gap), F32), sin, jnp.zeros((n, gap), F32)], axis=1)
    return jnp.stack([c, s])


def _attention(proj, proj_meta, sinks, qg, kg, batch, seq, col_q, col_k, col_v):
    nb = seq // ATT_BLOCK
    t = batch * seq
    tab = _rope_tables(jnp.arange(seq) + N_META)
    tabm = _rope_tables(jnp.arange(N_META))
    kb, vb = col_k // KV_COLS, col_v // KV_COLS
    kv_scratch = pltpu.VMEM((ATT_KV_HEADS, ATT_BLOCK, HEAD_DIM), BF16)
    return pl.pallas_call(
        _attn_kernel,
        grid=(batch, nb),
        in_specs=[
            pl.BlockSpec(memory_space=pltpu.SMEM),
            pl.BlockSpec((ATT_BLOCK, Q_COLS), lambda b, n: (b * nb + n, col_q // Q_COLS)),
            pl.BlockSpec((ATT_BLOCK, KV_COLS), lambda b, n: (b * nb + n, kb)),
            pl.BlockSpec((ATT_BLOCK, KV_COLS), lambda b, n: (b * nb + n, vb)),
            pl.BlockSpec((N_META, KV_COLS), lambda b, n: (0, kb)),
            pl.BlockSpec((N_META, KV_COLS), lambda b, n: (0, vb)),
            pl.BlockSpec((1, HEAD_DIM), lambda b, n: (0, 0)),
            pl.BlockSpec((1, HEAD_DIM), lambda b, n: (0, 0)),
            pl.BlockSpec((2, ATT_BLOCK, HEAD_DIM), lambda b, n: (0, n, 0)),
            pl.BlockSpec((2, N_META, HEAD_DIM), lambda b, n: (0, 0, 0)),
        ],
        out_specs=pl.BlockSpec((ATT_BLOCK, Q_COLS), lambda b, n: (b * nb + n, 0)),
        out_shape=jax.ShapeDtypeStruct((t, Q_COLS), BF16),
        scratch_shapes=[kv_scratch, kv_scratch, kv_scratch, kv_scratch],
        compiler_params=_params("arbitrary", "arbitrary"),
        name="swa_attention",
    )(sinks, proj, proj, proj, proj_meta, proj_meta,
      qg[ROPE_LANE_ORDER].reshape(1, HEAD_DIM), kg[ROPE_LANE_ORDER].reshape(1, HEAD_DIM), tab, tabm)


def _cumsum_rows(x):
    n = x.shape[0]
    row = lax.broadcasted_iota(jnp.int32, x.shape, 0)
    shift = 1
    while shift < n:
        x = x + jnp.where(row >= shift, pltpu.roll(x, shift, 0), 0.0)
        shift *= 2
    return x


def _hgrn_chunk(hq, hf, hi, hg, lb, ng, st_ref, sc_ref, bs_ref, ks_ref):
    c = hq.shape[0]
    f = lb + (1.0 - lb) * _sigmoid(hf)
    logf = jnp.log(f)
    kk = (1.0 - lb) * _sigmoid(-hf)
    q = hq * _sigmoid(hq)
    b = _cumsum_rows(logf)
    b_last = b[c - 1:c, :]
    b_mid = b[c // 2 - 1:c // 2, :]
    q_mid = (q * jnp.exp(jnp.minimum(b - b_mid, HGRN_EXP_CLAMP))).astype(BF16)
    k_mid = (kk * jnp.exp(jnp.minimum(b_mid - b, HGRN_EXP_CLAMP))).astype(BF16)
    q_in = (q * jnp.exp(b)).astype(BF16)
    k_out = (kk * jnp.exp(b_last - b)).astype(BF16)
    decay = jnp.exp(b_last)
    v = hi.astype(BF16)
    causal = (lax.broadcasted_iota(jnp.int32, (c, c), 0) >= lax.broadcasted_iota(jnp.int32, (c, c), 1))
    nt = (((1,), (1,)), ((), ()))
    tn = (((0,), (0,)), ((), ()))
    for h in range(HGRN_HEADS):
        sl = slice(h * HGRN_DK, (h + 1) * HGRN_DK)
        sc_ref[h] = lax.dot_general(q_mid[:, sl], k_mid[:, sl], nt, preferred_element_type=F32)

    @pl.when(jnp.max(jnp.maximum(b[0:1, :] - b_mid, b_mid - b_last)) > HGRN_EXP_CLAMP)
    def _():
        bs_ref[...] = b
        ks_ref[...] = kk
        col_id = lax.broadcasted_iota(jnp.int32, (c, c), 1)

        def column(s, accs):
            w = q * jnp.exp(jnp.minimum(b - bs_ref[pl.ds(s, 1), :], 0.0)) * ks_ref[pl.ds(s, 1), :]
            return tuple(
                jnp.where(col_id == s, jnp.sum(w[:, h * HGRN_DK:(h + 1) * HGRN_DK], axis=1, keepdims=True), acc)
                for h, acc in enumerate(accs))

        accs = lax.fori_loop(0, c, column, tuple(jnp.zeros((c, c), F32) for _ in range(HGRN_HEADS)))
        for h in range(HGRN_HEADS):
            sc_ref[h] = accs[h]

    outs = []
    for h in range(HGRN_HEADS):
        sl = slice(h * HGRN_DK, (h + 1) * HGRN_DK)
        st = st_ref[h]
        scores = jnp.where(causal, sc_ref[h], 0.0).astype(BF16)
        o = (jnp.dot(scores, v[:, sl], preferred_element_type=F32)
             + lax.dot_general(q_in[:, sl], st.astype(BF16), nt, preferred_element_type=F32))
        st_ref[h] = (st * decay[:, sl]
                     + lax.dot_general(v[:, sl], k_out[:, sl], tn, preferred_element_type=F32))
        outs.append(_rms(o, ng[:, sl]))
    o_all = jnp.concatenate(outs, axis=1)
    return o_all * (hg * _sigmoid(hg))


def _lower_bound(lbl_ref):
    l0, l1 = lbl_ref[0:1, :], lbl_ref[1:2, :]
    m = jnp.maximum(l0, l1)
    e0, e1 = jnp.exp(l0 - m), jnp.exp(l1 - m)
    return e0 / (e0 + e1)


def _hgrn_meta_kernel(hq_ref, hf_ref, hi_ref, hg_ref, lbl_ref, ng_ref, st_out_ref, sc_ref, bs_ref, ks_ref):
    st_out_ref[...] = jnp.zeros_like(st_out_ref)
    _hgrn_chunk(hq_ref[...], hf_ref[...], hi_ref[...], hg_ref[...], _lower_bound(lbl_ref), ng_ref[...],
                st_out_ref, sc_ref, bs_ref, ks_ref)


def _hgrn_work_buffers(c):
    return [pltpu.VMEM((HGRN_HEADS, c, c), F32), pltpu.VMEM((c, HK), F32), pltpu.VMEM((c, HK), F32)]


def _hgrn_kernel(hq_ref, hf_ref, hi_ref, hg_ref, lbl_ref, ng_ref, st0_ref, o_ref, st_ref, sc_ref, bs_ref, ks_ref,
                 *, chunks):
    @pl.when(pl.program_id(1) == 0)
    def _():
        st_ref[...] = st0_ref[...]

    lb = _lower_bound(lbl_ref)

    def body(ci, carry):
        rows = pl.ds(pl.multiple_of(ci * HGRN_CHUNK, HGRN_CHUNK), HGRN_CHUNK)
        o = _hgrn_chunk(hq_ref[rows, :], hf_ref[rows, :], hi_ref[rows, :], hg_ref[rows, :], lb, ng_ref[...],
                        st_ref, sc_ref, bs_ref, ks_ref)
        o_ref[rows, :] = o.astype(BF16)
        return carry

    lax.fori_loop(0, chunks, body, 0)


def _hgrn(proj, proj_meta, lb_logits, norm_g, batch, seq, col_hq, rows_per_step):
    t = batch * seq
    cb = col_hq // HK
    state_shape = (HGRN_HEADS, HGRN_DK, HGRN_DK)
    vec = pl.BlockSpec((1, HK), lambda *_: (0, 0))
    lbs = pl.BlockSpec((2, HK), lambda *_: (0, 0))
    st_meta = pl.pallas_call(
        _hgrn_meta_kernel,
        grid=(1,),
        in_specs=[pl.BlockSpec((N_META, HK), lambda i, k=k: (0, cb + k)) for k in range(4)] + [lbs, vec],
        out_specs=pl.BlockSpec(state_shape, lambda i: (0, 0, 0)),
        out_shape=jax.ShapeDtypeStruct(state_shape, F32),
        scratch_shapes=_hgrn_work_buffers(N_META),
        compiler_params=_params("arbitrary"),
        name="hgrn2_meta",
    )(proj_meta, proj_meta, proj_meta, proj_meta, lb_logits, norm_g.reshape(1, HK))
    steps = seq // rows_per_step
    return pl.pallas_call(
        functools.partial(_hgrn_kernel, chunks=rows_per_step // HGRN_CHUNK),
        grid=(batch, steps),
        in_specs=[pl.BlockSpec((rows_per_step, HK), lambda b, s, k=k: (b * steps + s, cb + k)) for k in range(4)]
        + [lbs, vec, pl.BlockSpec(state_shape, lambda b, s: (0, 0, 0))],
        out_specs=pl.BlockSpec((rows_per_step, HK), lambda b, s: (b * steps + s, 0)),
        out_shape=jax.ShapeDtypeStruct((t, HK), BF16),
        scratch_shapes=[pltpu.VMEM(state_shape, F32)] + _hgrn_work_buffers(HGRN_CHUNK),
        compiler_params=_params("arbitrary", "arbitrary"),
        name="hgrn2",
    )(proj, proj, proj, proj, lb_logits, norm_g.reshape(1, HK), st_meta)


def _out_proj_kernel(x_ref, att_ref, rec_ref, wa_ref, wr_ref, g_ref, h_ref, xnt_ref):
    h = (x_ref[...]
         + jnp.dot(att_ref[...], wa_ref[...], preferred_element_type=F32)
         + jnp.dot(rec_ref[...], wr_ref[...], preferred_element_type=F32))
    h_ref[...] = h
    xnt_ref[...] = jnp.transpose(_rms(h, g_ref[...])).astype(BF16)


def _out_proj(x2d, att, rec, wa, wr, g, tm):
    t, d = x2d.shape
    return pl.pallas_call(
        _out_proj_kernel,
        grid=(t // tm,),
        in_specs=[
            pl.BlockSpec((tm, d), lambda i: (i, 0)),
            pl.BlockSpec((tm, att.shape[1]), lambda i: (i, 0)),
            pl.BlockSpec((tm, rec.shape[1]), lambda i: (i, 0)),
            pl.BlockSpec(wa.shape, lambda i: (0, 0)),
            pl.BlockSpec(wr.shape, lambda i: (0, 0)),
            pl.BlockSpec((1, d), lambda i: (0, 0)),
        ],
        out_specs=[pl.BlockSpec((tm, d), lambda i: (i, 0)), pl.BlockSpec((d, tm), lambda i: (0, i))],
        out_shape=[jax.ShapeDtypeStruct((t, d), F32), jax.ShapeDtypeStruct((d, t), BF16)],
        compiler_params=_params("arbitrary"),
        name="out_proj",
    )(x2d, att, rec, wa, wr, g.reshape(1, d))


def _sort_desc(v):
    v = list(v)
    n = len(v)
    k = 2
    while k <= n:
        j = k // 2
        while j >= 1:
            for i in range(n):
                l = i ^ j
                if l > i:
                    hi, lo = jnp.maximum(v[i], v[l]), jnp.minimum(v[i], v[l])
                    v[i], v[l] = (hi, lo) if (i & k) == 0 else (lo, hi)
            j //= 2
        k *= 2
    return v


def _merge_desc(v):
    v = list(v)
    j = len(v) // 2
    while j >= 1:
        for i in range(len(v)):
            l = i ^ j
            if l > i:
                v[i], v[l] = jnp.maximum(v[i], v[l]), jnp.minimum(v[i], v[l])
        j //= 2
    return v


def _top_merge(a, b):
    n = len(a)
    return _merge_desc([jnp.maximum(a[i], b[n - 1 - i]) for i in range(n)])


def _top_products(w1, w2):
    k = PEER_TOPK
    rows = [[w1[i] * w2[j] for j in range(k // (i + 1))] for i in range(k // 2)]
    tail = [w1[i] * w2[0] for i in range(k // 2, k)]
    b1 = _merge_desc(rows[1] + tail[::-1])
    b2 = _sort_desc(rows[2] + rows[3] + rows[4] + rows[5] + rows[6])
    t = _top_merge(_top_merge(rows[0], b1), b2)
    t[k - 1] = jnp.maximum(t[k - 1], rows[7][0])
    t[k - 2] = jnp.maximum(t[k - 2], rows[7][1])
    return _merge_desc(t)


def _top16_of_rows(s_ref, c):
    blocks = []
    for b in range(PEER_NKEYS // PEER_TOPK):
        blocks.append(_sort_desc([s_ref[c, (b * PEER_TOPK + k) * 8:(b * PEER_TOPK + k + 1) * 8, :]
                                  for k in range(PEER_TOPK)]))
    while len(blocks) > 1:
        blocks = [_top_merge(blocks[i], blocks[i + 1]) for i in range(0, len(blocks), 2)]
    return blocks[0]


def _peer_keys_kernel(xt_ref, wqt_ref, keys_ref, a1_ref, a2_ref, tau_ref, s_even, s_odd):
    tm = xt_ref.shape[1]
    groups = tm // LANES
    bufs = (s_even, s_odd)
    if groups < 8:
        for s_ref in bufs:
            s_ref[...] = jnp.zeros_like(s_ref)

    def scores(h, slot):
        s_ref = bufs[slot]
        r0 = pl.multiple_of(h * 2 * PEER_DHALF, 2 * PEER_DHALF)
        q = jnp.dot(wqt_ref[pl.ds(r0, 2 * PEER_DHALF), :], xt_ref[...], preferred_element_type=F32).astype(BF16)
        s1 = jnp.dot(keys_ref[h, 0], q[:PEER_DHALF], preferred_element_type=F32)
        s2 = jnp.dot(keys_ref[h, 1], q[PEER_DHALF:], preferred_element_type=F32)
        a2_ref[h] = jnp.exp(s2 - jnp.max(s2, axis=0, keepdims=True))
        for g in range(groups):
            s_ref[0, pl.ds(g, PEER_NKEYS, stride=8), :] = s1[:, g * LANES:(g + 1) * LANES]
            s_ref[1, pl.ds(g, PEER_NKEYS, stride=8), :] = s2[:, g * LANES:(g + 1) * LANES]

    def select(h, slot):
        s_ref = bufs[slot]
        t1 = _top16_of_rows(s_ref, 0)
        t2 = _top16_of_rows(s_ref, 1)
        w1 = [jnp.exp(t - t1[0]) for t in t1]
        w2 = [jnp.exp(t - t2[0]) for t in t2]
        top = _top_products(w1, w2)
        z = top[0]
        for m in top[1:]:
            z = z + m
        inv_z = 1.0 / z
        tau_ref[h] = _top_products([w * inv_z for w in w1], w2)[PEER_TOPK - 1][:groups]
        for k in range(PEER_NKEYS):
            a1_ref[h, k] = (jnp.exp(s_ref[0, k * 8:(k + 1) * 8, :] - t1[0]) * inv_z)[:groups]

    def body(j, carry):
        h = 2 * j
        scores(h, 0)
        scores(h + 1, 1)
        select(h, 0)
        select(h + 1, 1)
        return carry

    lax.fori_loop(0, PEER_HEADS // 2, body, 0)


def _peer_keys(xnt, wqt, keys, tm):
    d, t = xnt.shape
    dq = wqt.shape[0]
    groups = tm // LANES
    return pl.pallas_call(
        _peer_keys_kernel,
        grid=(t // tm,),
        in_specs=[
            pl.BlockSpec((d, tm), lambda i: (0, i)),
            pl.BlockSpec((dq, d), lambda i: (0, 0)),
            pl.BlockSpec(keys.shape, lambda i: (0, 0, 0, 0)),
        ],
        out_specs=[
            pl.BlockSpec((PEER_HEADS, PEER_NKEYS, groups, LANES), lambda i: (0, 0, i, 0)),
            pl.BlockSpec((PEER_HEADS, PEER_NKEYS, tm), lambda i: (0, 0, i)),
            pl.BlockSpec((PEER_HEADS, groups, LANES), lambda i: (0, i, 0)),
        ],
        out_shape=[
            jax.ShapeDtypeStruct((PEER_HEADS, PEER_NKEYS, t // LANES, LANES), F32),
            jax.ShapeDtypeStruct((PEER_HEADS, PEER_NKEYS, t), F32),
            jax.ShapeDtypeStruct((PEER_HEADS, t // LANES, LANES), F32),
        ],
        scratch_shapes=[pltpu.VMEM((2, PEER_NKEYS * 8, LANES), F32)] * 2,
        compiler_params=_params("arbitrary"),
        name="peer_keys",
    )(xnt, wqt, keys)


def _gelu_tanh(x):
    c = math.sqrt(2.0 / math.pi)
    return x * (0.5 * (1.0 + jnp.tanh(c * (x + 0.044715 * (x * x * x)))))


def _peer_dense_kernel(xt_ref, u_ref, vt_ref, a1_ref, a2_ref, tau_ref, o_ref, a_even, a_odd, *, chunk, n_e):
    e = pl.program_id(1)
    eb, tm = a_even.shape
    keys_per_step = eb // PEER_NKEYS
    key_base = jnp.maximum(e - 1, 0) * keys_per_step
    bufs = (a_even, a_odd)

    def run(a_write, a_read):
        for q in range(tm // chunk):
            lanes = slice(q * chunk, (q + 1) * chunk)
            if a_write is not None:
                a_write[:, lanes] = jnp.dot(u_ref[...], xt_ref[:, lanes], preferred_element_type=F32)
            if a_read is None:
                continue
            tiles = []
            for k in range(keys_per_step):
                rows = slice(k * PEER_NKEYS, (k + 1) * PEER_NKEYS)
                cols = []
                for g in range(q * chunk // LANES, (q + 1) * chunk // LANES):
                    glanes = slice(g * LANES, (g + 1) * LANES)
                    gate = jnp.zeros((PEER_NKEYS, LANES), F32)
                    for h in range(PEER_HEADS):
                        p = a1_ref[h, pl.ds(key_base + k, 1), g, :] * a2_ref[h, :, glanes]
                        gate = gate + jnp.where(p >= tau_ref[h, g:g + 1, :], p, 0.0)
                    cols.append((_gelu_tanh(a_read[rows, glanes]) * gate).astype(BF16))
                tiles.append(jnp.concatenate(cols, axis=1))
            half = len(tiles) // 2
            h_lo = jnp.concatenate(tiles[:half], axis=0)
            h_hi = jnp.concatenate(tiles[half:], axis=0)
            o_ref[:, lanes] += (jnp.dot(vt_ref[:, :eb // 2], h_lo, preferred_element_type=F32)
                                + jnp.dot(vt_ref[:, eb // 2:], h_hi, preferred_element_type=F32))

    @pl.when(e == 0)
    def _():
        o_ref[...] = jnp.zeros_like(o_ref)
        run(a_even, None)

    for parity in range(2):
        @pl.when(jnp.logical_and(jnp.logical_and(e > 0, e < n_e), e % 2 == parity))
        def _():
            run(bufs[parity], bufs[1 - parity])

    @pl.when(e == n_e)
    def _():
        run(None, bufs[(n_e - 1) % 2])


def _peer_dense(xnt, u, vt, a1, a2, tau, tm, eb):
    d, t = xnt.shape
    n_e = u.shape[0] // eb
    groups = tm // LANES
    return pl.pallas_call(
        functools.partial(_peer_dense_kernel, chunk=min(tm, 256), n_e=n_e),
        grid=(t // tm, n_e + 1),
        in_specs=[
            pl.BlockSpec((d, tm), lambda i, e: (0, i)),
            pl.BlockSpec((eb, d), lambda i, e: (jnp.minimum(e, n_e - 1), 0)),
            pl.BlockSpec((d, eb), lambda i, e: (0, jnp.maximum(e - 1, 0))),
            pl.BlockSpec((PEER_HEADS, PEER_NKEYS, groups, LANES), lambda i, e: (0, 0, i, 0)),
            pl.BlockSpec((PEER_HEADS, PEER_NKEYS, tm), lambda i, e: (0, 0, i)),
            pl.BlockSpec((PEER_HEADS, groups, LANES), lambda i, e: (0, i, 0)),
        ],
        out_specs=pl.BlockSpec((d, tm), lambda i, e: (0, i)),
        out_shape=jax.ShapeDtypeStruct((d, t), F32),
        scratch_shapes=[pltpu.VMEM((eb, tm), F32), pltpu.VMEM((eb, tm), F32)],
        compiler_params=_params("arbitrary", "arbitrary"),
        name="peer_dense",
    )(xnt, u, vt, a1, a2, tau)


def _finish_kernel(h_ref, pt_ref, o_ref):
    o_ref[...] = h_ref[...] + jnp.transpose(pt_ref[...])


def _finish(h2, peer_t, tm):
    t, d = h2.shape
    return pl.pallas_call(
        _finish_kernel,
        grid=(t // tm,),
        in_specs=[pl.BlockSpec((tm, d), lambda i: (i, 0)), pl.BlockSpec((d, tm), lambda i: (0, i))],
        out_specs=pl.BlockSpec((tm, d), lambda i: (i, 0)),
        out_shape=jax.ShapeDtypeStruct((t, d), F32),
        compiler_params=_params("arbitrary"),
        name="finish",
    )(h2, peer_t)


def _tile(n, want):
    t = min(n, want)
    assert n % t == 0, (n, want)
    return t


def kernel(x, meta_tokens, hgrn_lb_logits, ln_mix_g, w_in, q_norm_g, k_norm_g, attn_sinks, hgrn_norm_g,
           w_out, ln_ffn_g, peer_w_q, peer_sub_keys, peer_u, peer_v):
    batch, seq, d = x.shape
    assert w_in.shape[0] == 1 and hgrn_lb_logits.shape[0] == 2, "single-layer trunk"
    assert seq % ATT_BLOCK == 0 and peer_sub_keys.shape[1:] == (PEER_HEADS, 2, PEER_NKEYS, PEER_DHALF)
    t = batch * seq
    x2d = x.reshape(t, d)

    wi = w_in[0]
    o_q, o_k, o_v, o_hq = 0, Q_COLS, Q_COLS + KV_COLS, Q_COLS + 2 * KV_COLS

    def rope_order(c0, heads):
        runs = [(0, ROPE_HALF), (ROPE_DIM, ROPE_DIM + HEAD_DIM // 2 - ROPE_HALF), (ROPE_HALF, ROPE_DIM),
                (ROPE_DIM + HEAD_DIM // 2 - ROPE_HALF, HEAD_DIM)]
        return [wi[:, c0 + h * HEAD_DIM + a:c0 + h * HEAD_DIM + b] for h in range(heads) for a, b in runs]

    w_perm = jnp.concatenate(rope_order(o_q, ATT_Q_HEADS) + [wi[:, o_hq:]] + rope_order(o_k, ATT_KV_HEADS)
                             + [wi[:, o_v:o_hq]], axis=1).astype(BF16)
    col_q, col_hq, col_k, col_v = 0, Q_COLS, Q_COLS + 4 * HK, Q_COLS + 4 * HK + KV_COLS

    proj = _in_proj(x2d, ln_mix_g[0], w_perm, _tile(t, 1024), IN_PROJ_TN)
    proj_meta = _in_proj(meta_tokens.astype(F32), ln_mix_g[0], w_perm, N_META, IN_PROJ_TN)

    att = _attention(proj, proj_meta, attn_sinks[0].astype(F32), q_norm_g[0], k_norm_g[0],
                     batch, seq, col_q, col_k, col_v)
    rec = _hgrn(proj, proj_meta, hgrn_lb_logits.astype(F32), hgrn_norm_g[0], batch, seq, col_hq,
                _tile(seq, 256))

    wo = w_out[0].astype(BF16)
    h2, xnt = _out_proj(x2d, att, rec, wo[:Q_COLS], wo[Q_COLS:], ln_ffn_g[0], _tile(t, 256))

    wqt = jnp.transpose(peer_w_q[0]).astype(BF16)
    keys = peer_sub_keys[0].astype(BF16)
    a1, a2, tau = _peer_keys(xnt, wqt, keys, _tile(t, 1024))

    u = peer_u[0].astype(BF16)
    vt = jnp.transpose(peer_v[0]).astype(BF16)
    peer_t = _peer_dense(xnt, u, vt, a1, a2, tau, _tile(t, 1024), 512)

    out = _finish(h2, peer_t, _tile(t, 256))
    return out.reshape(batch, seq, d)
```

```python
import functools
import math

import jax
import jax.numpy as jnp
import numpy as np
from jax import lax
from jax.experimental import pallas as pl
from jax.experimental.pallas import tpu as pltpu

F32 = jnp.float32
BF16 = jnp.bfloat16

N_META = 16
RMS_EPS = 1e-6
HEAD_DIM = 128
ATT_Q_HEADS = 8
ATT_KV_HEADS = 2
ATT_GROUP = ATT_Q_HEADS // ATT_KV_HEADS
ATT_BLOCK = 128
ROPE_THETA = 500000.0
ROPE_DIM = HEAD_DIM // 4
HGRN_HEADS = 8
HGRN_DK = 128
HGRN_CHUNK = 64
Q_COLS = ATT_Q_HEADS * HEAD_DIM
KV_COLS = ATT_KV_HEADS * HEAD_DIM
HK = HGRN_HEADS * HGRN_DK
IN_COLS = Q_COLS + 2 * KV_COLS + 4 * HK
IN_PROJ_TN = IN_COLS // 4
PEER_HEADS = 8
PEER_NKEYS = 128
PEER_DHALF = 128
PEER_TOPK = 16

V7X_VMEM_BYTES = 64 * 1024 * 1024
VMEM_LIMIT = V7X_VMEM_BYTES - 8 * 1024 * 1024
LANES = 128
HGRN_EXP_CLAMP = 85.0
MASKED = -1e30


def _params(*sem):
    return pltpu.CompilerParams(dimension_semantics=sem, vmem_limit_bytes=VMEM_LIMIT)


def _rms(x, g):
    ms = jnp.mean(x * x, axis=-1, keepdims=True)
    return x * lax.rsqrt(ms + RMS_EPS) * g


def _sigmoid(x):
    return 1.0 / (1.0 + jnp.exp(-x))


def _in_proj_kernel(x_ref, g_ref, w_ref, o_ref, xn_ref):
    @pl.when(pl.program_id(1) == 0)
    def _():
        xn_ref[...] = _rms(x_ref[...], g_ref[...]).astype(BF16)

    o_ref[...] = jnp.dot(xn_ref[...], w_ref[...], preferred_element_type=F32)


def _in_proj(x2d, g, w, tm, tn):
    m, d = x2d.shape
    n = w.shape[1]
    return pl.pallas_call(
        _in_proj_kernel,
        grid=(m // tm, n // tn),
        in_specs=[
            pl.BlockSpec((tm, d), lambda i, j: (i, 0)),
            pl.BlockSpec((1, d), lambda i, j: (0, 0)),
            pl.BlockSpec((d, tn), lambda i, j: (0, j)),
        ],
        out_specs=pl.BlockSpec((tm, tn), lambda i, j: (i, j)),
        out_shape=jax.ShapeDtypeStruct((m, n), F32),
        scratch_shapes=[pltpu.VMEM((tm, d), BF16)],
        compiler_params=_params("arbitrary", "arbitrary"),
        name="in_proj",
    )(x2d, g.reshape(1, d), w)


ROPE_HALF = ROPE_DIM // 2
ROPE_LANE_ORDER = np.concatenate([
    np.arange(0, ROPE_HALF), np.arange(ROPE_DIM, ROPE_DIM + HEAD_DIM // 2 - ROPE_HALF),
    np.arange(ROPE_HALF, ROPE_DIM), np.arange(ROPE_DIM + HEAD_DIM // 2 - ROPE_HALF, HEAD_DIM)])


def _rope(x, tab_ref):
    return x * tab_ref[0] + pltpu.roll(x, HEAD_DIM // 2, 1) * tab_ref[1]


def _attn_kernel(sink_ref, q_ref, k_ref, v_ref, km_ref, vm_ref, qg_ref, kg_ref, tab_ref, tabm_ref, o_ref,
                 kprev_s, vprev_s, kmeta_s, vmeta_s):
    b, n = pl.program_id(0), pl.program_id(1)
    blk = ATT_BLOCK
    rows = ATT_GROUP * blk
    scale = HEAD_DIM ** -0.5

    @pl.when(jnp.logical_and(b == 0, n == 0))
    def _():
        pad = jnp.zeros((blk - N_META, HEAD_DIM), BF16)
        for g in range(ATT_KV_HEADS):
            ksl = slice(g * HEAD_DIM, (g + 1) * HEAD_DIM)
            k_meta = _rope(_rms(km_ref[:, ksl], kg_ref[...]), tabm_ref).astype(BF16)
            kmeta_s[g] = jnp.concatenate([k_meta, pad], axis=0)
            vmeta_s[g] = jnp.concatenate([vm_ref[:, ksl].astype(BF16), pad], axis=0)

    @pl.when(n == 0)
    def _():
        kprev_s[...] = jnp.zeros_like(kprev_s)
        vprev_s[...] = jnp.zeros_like(vprev_s)

    qi = lax.broadcasted_iota(jnp.int32, (rows, blk), 0) & (blk - 1)
    ki = lax.broadcasted_iota(jnp.int32, (rows, blk), 1)
    own_mask = ki <= qi
    prev_mask = ki > jnp.where(n > 0, qi, blk)
    meta_mask = ki < N_META
    head_row = lax.shift_right_logical(lax.broadcasted_iota(jnp.int32, (rows, 1), 0), 7)
    nt = (((1,), (1,)), ((), ()))
    for g in range(ATT_KV_HEADS):
        ksl = slice(g * HEAD_DIM, (g + 1) * HEAD_DIM)
        k_own = _rope(_rms(k_ref[:, ksl], kg_ref[...]), tab_ref).astype(BF16)
        v_own = v_ref[:, ksl].astype(BF16)
        keys = jnp.concatenate([kprev_s[g], k_own, kmeta_s[g]], axis=0)
        vals = jnp.concatenate([vprev_s[g], v_own, vmeta_s[g]], axis=0)
        kprev_s[g] = k_own
        vprev_s[g] = v_own
        qs = []
        sink = jnp.zeros((rows, 1), F32)
        for hh in range(ATT_GROUP):
            h = g * ATT_GROUP + hh
            qs.append(_rope(_rms(q_ref[:, h * HEAD_DIM:(h + 1) * HEAD_DIM], qg_ref[...]), tab_ref))
            sink = jnp.where(head_row == hh, sink_ref[h], sink)
        q = jnp.concatenate(qs, axis=0).astype(BF16)
        s = lax.dot_general(q, keys, nt, preferred_element_type=F32) * scale
        s_prev = jnp.where(prev_mask, s[:, :blk], MASKED)
        s_own = jnp.where(own_mask, s[:, blk:2 * blk], MASKED)
        s_meta = jnp.where(meta_mask, s[:, 2 * blk:], MASKED)
        m = jnp.maximum(jnp.max(jnp.maximum(jnp.maximum(s_prev, s_own), s_meta), axis=-1, keepdims=True), sink)
        p_prev, p_own, p_meta = jnp.exp(s_prev - m), jnp.exp(s_own - m), jnp.exp(s_meta - m)
        denom = jnp.sum(p_prev + p_own + p_meta, axis=-1, keepdims=True) + jnp.exp(sink - m)
        p = jnp.concatenate([p_prev, p_own, p_meta], axis=1).astype(BF16)
        out = jnp.dot(p, vals, preferred_element_type=F32) / denom
        for hh in range(ATT_GROUP):
            h = g * ATT_GROUP + hh
            o_ref[:, h * HEAD_DIM:(h + 1) * HEAD_DIM] = out[hh * blk:(hh + 1) * blk].astype(BF16)


def _rope_tables(pos):
    inv = ROPE_THETA ** (-jnp.arange(0, ROPE_DIM, 2, dtype=F32) / ROPE_DIM)
    ang = pos.astype(F32)[:, None] * inv[None, :]
    cos, sin = jnp.cos(ang), jnp.sin(ang)
    n = pos.shape[0]
    gap = HEAD_DIM // 2 - ROPE_HALF
    c = jnp.concatenate([cos, jnp.ones((n, gap), F32), cos, jnp.ones((n, gap), F32)], axis=1)
    s = jnp.concatenate([-sin, jnp.zeros((n, gap), F32), sin, jnp.zeros((n, gap), F32)], axis=1)
    return jnp.stack([c, s])


def _attention(proj, proj_meta, sinks, qg, kg, batch, seq, col_q, col_k, col_v):
    nb = seq // ATT_BLOCK
    t = batch * seq
    tab = _rope_tables(jnp.arange(seq) + N_META)
    tabm = _rope_tables(jnp.arange(N_META))
    kb, vb = col_k // KV_COLS, col_v // KV_COLS
    kv_scratch = pltpu.VMEM((ATT_KV_HEADS, ATT_BLOCK, HEAD_DIM), BF16)
    return pl.pallas_call(
        _attn_kernel,
        grid=(batch, nb),
        in_specs=[
            pl.BlockSpec(memory_space=pltpu.SMEM),
            pl.BlockSpec((ATT_BLOCK, Q_COLS), lambda b, n: (b * nb + n, col_q // Q_COLS)),
            pl.BlockSpec((ATT_BLOCK, KV_COLS), lambda b, n: (b * nb + n, kb)),
            pl.BlockSpec((ATT_BLOCK, KV_COLS), lambda b, n: (b * nb + n, vb)),
            pl.BlockSpec((N_META, KV_COLS), lambda b, n: (0, kb)),
            pl.BlockSpec((N_META, KV_COLS), lambda b, n: (0, vb)),
            pl.BlockSpec((1, HEAD_DIM), lambda b, n: (0, 0)),
            pl.BlockSpec((1, HEAD_DIM), lambda b, n: (0, 0)),
            pl.BlockSpec((2, ATT_BLOCK, HEAD_DIM), lambda b, n: (0, n, 0)),
            pl.BlockSpec((2, N_META, HEAD_DIM), lambda b, n: (0, 0, 0)),
        ],
        out_specs=pl.BlockSpec((ATT_BLOCK, Q_COLS), lambda b, n: (b * nb + n, 0)),
        out_shape=jax.ShapeDtypeStruct((t, Q_COLS), BF16),
        scratch_shapes=[kv_scratch, kv_scratch, kv_scratch, kv_scratch],
        compiler_params=_params("arbitrary", "arbitrary"),
        name="swa_attention",
    )(sinks, proj, proj, proj, proj_meta, proj_meta,
      qg[ROPE_LANE_ORDER].reshape(1, HEAD_DIM), kg[ROPE_LANE_ORDER].reshape(1, HEAD_DIM), tab, tabm)


def _cumsum_rows(x):
    n = x.shape[0]
    row = lax.broadcasted_iota(jnp.int32, x.shape, 0)
    shift = 1
    while shift < n:
        x = x + jnp.where(row >= shift, pltpu.roll(x, shift, 0), 0.0)
        shift *= 2
    return x


def _hgrn_chunk(hq, hf, hi, hg, lb, ng, st_ref, sc_ref, bs_ref, ks_ref):
    c = hq.shape[0]
    half = c // 2

    def gates():
        f = lb + (1.0 - lb) * _sigmoid(hf)
        return hq * _sigmoid(hq), (1.0 - lb) * _sigmoid(-hf), _cumsum_rows(jnp.log(f))

    steep = jnp.maximum(-hf, 0.0)
    bound = half * math.log(2.0) + jnp.maximum(jnp.sum(steep[:half], axis=0, keepdims=True),
                                               jnp.sum(steep[half:], axis=0, keepdims=True))
    direct = jnp.max(bound) > HGRN_EXP_CLAMP

    @pl.when(direct)
    def _():
        q, kk, b = gates()
        bs_ref[...] = b
        ks_ref[...] = kk
        col_id = lax.broadcasted_iota(jnp.int32, (c, c), 1)

        def column(s, accs):
            w = q * jnp.exp(jnp.minimum(b - bs_ref[pl.ds(s, 1), :], 0.0)) * ks_ref[pl.ds(s, 1), :]
            return tuple(
                jnp.where(col_id == s, jnp.sum(w[:, h * HGRN_DK:(h + 1) * HGRN_DK], axis=1, keepdims=True), acc)
                for h, acc in enumerate(accs))

        accs = lax.fori_loop(0, c, column, tuple(jnp.zeros((c, c), F32) for _ in range(HGRN_HEADS)))
        for h in range(HGRN_HEADS):
            sc_ref[h] = accs[h]

    q, kk, b = gates()
    b_last = b[c - 1:c, :]
    b_mid = b[half - 1:half, :]
    q_mid = (q * jnp.exp(jnp.minimum(b - b_mid, HGRN_EXP_CLAMP))).astype(BF16)
    k_mid = (kk * jnp.exp(jnp.minimum(b_mid - b, HGRN_EXP_CLAMP))).astype(BF16)
    q_in = (q * jnp.exp(b)).astype(BF16)
    k_out = (kk * jnp.exp(b_last - b)).astype(BF16)
    decay = jnp.exp(b_last)
    v = hi.astype(BF16)
    causal = (lax.broadcasted_iota(jnp.int32, (c, c), 0) >= lax.broadcasted_iota(jnp.int32, (c, c), 1))
    nt = (((1,), (1,)), ((), ()))
    tn = (((0,), (0,)), ((), ()))
    outs = []
    for h in range(HGRN_HEADS):
        sl = slice(h * HGRN_DK, (h + 1) * HGRN_DK)
        st = st_ref[h]
        scores = lax.dot_general(q_mid[:, sl], k_mid[:, sl], nt, preferred_element_type=F32)
        scores = jnp.where(causal, jnp.where(direct, sc_ref[h], scores), 0.0).astype(BF16)
        o = (jnp.dot(scores, v[:, sl], preferred_element_type=F32)
             + lax.dot_general(q_in[:, sl], st.astype(BF16), nt, preferred_element_type=F32))
        st_ref[h] = (st * decay[:, sl]
                     + lax.dot_general(v[:, sl], k_out[:, sl], tn, preferred_element_type=F32))
        outs.append(_rms(o, ng[:, sl]))
    o_all = jnp.concatenate(outs, axis=1)
    return o_all * (hg * _sigmoid(hg))


def _lower_bound(lbl_ref):
    l0, l1 = lbl_ref[0:1, :], lbl_ref[1:2, :]
    m = jnp.maximum(l0, l1)
    e0, e1 = jnp.exp(l0 - m), jnp.exp(l1 - m)
    return e0 / (e0 + e1)


def _hgrn_meta_kernel(hq_ref, hf_ref, hi_ref, hg_ref, lbl_ref, ng_ref, st_out_ref, sc_ref, bs_ref, ks_ref):
    st_out_ref[...] = jnp.zeros_like(st_out_ref)
    sc_ref[...] = jnp.zeros_like(sc_ref)
    _hgrn_chunk(hq_ref[...], hf_ref[...], hi_ref[...], hg_ref[...], _lower_bound(lbl_ref), ng_ref[...],
                st_out_ref, sc_ref, bs_ref, ks_ref)


def _hgrn_work_buffers(c):
    return [pltpu.VMEM((HGRN_HEADS, c, c), F32), pltpu.VMEM((c, HK), F32), pltpu.VMEM((c, HK), F32)]


def _hgrn_kernel(hq_ref, hf_ref, hi_ref, hg_ref, lbl_ref, ng_ref, st0_ref, o_ref, st_ref, sc_ref, bs_ref, ks_ref,
                 *, chunks):
    @pl.when(pl.program_id(1) == 0)
    def _():
        st_ref[...] = st0_ref[...]
        sc_ref[...] = jnp.zeros_like(sc_ref)

    lb = _lower_bound(lbl_ref)

    def body(ci, carry):
        rows = pl.ds(pl.multiple_of(ci * HGRN_CHUNK, HGRN_CHUNK), HGRN_CHUNK)
        o = _hgrn_chunk(hq_ref[rows, :], hf_ref[rows, :], hi_ref[rows, :], hg_ref[rows, :], lb, ng_ref[...],
                        st_ref, sc_ref, bs_ref, ks_ref)
        o_ref[rows, :] = o.astype(BF16)
        return carry

    lax.fori_loop(0, chunks, body, 0)


def _hgrn(proj, proj_meta, lb_logits, norm_g, batch, seq, col_hq, rows_per_step):
    t = batch * seq
    cb = col_hq // HK
    state_shape = (HGRN_HEADS, HGRN_DK, HGRN_DK)
    vec = pl.BlockSpec((1, HK), lambda *_: (0, 0))
    lbs = pl.BlockSpec((2, HK), lambda *_: (0, 0))
    st_meta = pl.pallas_call(
        _hgrn_meta_kernel,
        grid=(1,),
        in_specs=[pl.BlockSpec((N_META, HK), lambda i, k=k: (0, cb + k)) for k in range(4)] + [lbs, vec],
        out_specs=pl.BlockSpec(state_shape, lambda i: (0, 0, 0)),
        out_shape=jax.ShapeDtypeStruct(state_shape, F32),
        scratch_shapes=_hgrn_work_buffers(N_META),
        compiler_params=_params("arbitrary"),
        name="hgrn2_meta",
    )(proj_meta, proj_meta, proj_meta, proj_meta, lb_logits, norm_g.reshape(1, HK))
    steps = seq // rows_per_step
    return pl.pallas_call(
        functools.partial(_hgrn_kernel, chunks=rows_per_step // HGRN_CHUNK),
        grid=(batch, steps),
        in_specs=[pl.BlockSpec((rows_per_step, HK), lambda b, s, k=k: (b * steps + s, cb + k)) for k in range(4)]
        + [lbs, vec, pl.BlockSpec(state_shape, lambda b, s: (0, 0, 0))],
        out_specs=pl.BlockSpec((rows_per_step, HK), lambda b, s: (b * steps + s, 0)),
        out_shape=jax.ShapeDtypeStruct((t, HK), BF16),
        scratch_shapes=[pltpu.VMEM(state_shape, F32)] + _hgrn_work_buffers(HGRN_CHUNK),
        compiler_params=_params("arbitrary", "arbitrary"),
        name="hgrn2",
    )(proj, proj, proj, proj, lb_logits, norm_g.reshape(1, HK), st_meta)


def _out_proj_kernel(x_ref, att_ref, rec_ref, wa_ref, wr_ref, g_ref, h_ref, xnt_ref):
    h = (x_ref[...]
         + jnp.dot(att_ref[...], wa_ref[...], preferred_element_type=F32)
         + jnp.dot(rec_ref[...], wr_ref[...], preferred_element_type=F32))
    h_ref[...] = h
    xnt_ref[...] = jnp.transpose(_rms(h, g_ref[...])).astype(BF16)


def _out_proj(x2d, att, rec, wa, wr, g, tm):
    t, d = x2d.shape
    return pl.pallas_call(
        _out_proj_kernel,
        grid=(t // tm,),
        in_specs=[
            pl.BlockSpec((tm, d), lambda i: (i, 0)),
            pl.BlockSpec((tm, att.shape[1]), lambda i: (i, 0)),
            pl.BlockSpec((tm, rec.shape[1]), lambda i: (i, 0)),
            pl.BlockSpec(wa.shape, lambda i: (0, 0)),
            pl.BlockSpec(wr.shape, lambda i: (0, 0)),
            pl.BlockSpec((1, d), lambda i: (0, 0)),
        ],
        out_specs=[pl.BlockSpec((tm, d), lambda i: (i, 0)), pl.BlockSpec((d, tm), lambda i: (0, i))],
        out_shape=[jax.ShapeDtypeStruct((t, d), F32), jax.ShapeDtypeStruct((d, t), BF16)],
        compiler_params=_params("arbitrary"),
        name="out_proj",
    )(x2d, att, rec, wa, wr, g.reshape(1, d))


def _sort_desc(v):
    v = list(v)
    n = len(v)
    k = 2
    while k <= n:
        j = k // 2
        while j >= 1:
            for i in range(n):
                l = i ^ j
                if l > i:
                    hi, lo = jnp.maximum(v[i], v[l]), jnp.minimum(v[i], v[l])
                    v[i], v[l] = (hi, lo) if (i & k) == 0 else (lo, hi)
            j //= 2
        k *= 2
    return v


def _merge_desc(v):
    v = list(v)
    j = len(v) // 2
    while j >= 1:
        for i in range(len(v)):
            l = i ^ j
            if l > i:
                v[i], v[l] = jnp.maximum(v[i], v[l]), jnp.minimum(v[i], v[l])
        j //= 2
    return v


def _top_merge(a, b):
    n = len(a)
    return _merge_desc([jnp.maximum(a[i], b[n - 1 - i]) for i in range(n)])


def _top_products(w1, w2):
    k = PEER_TOPK
    rows = [[w1[i] * w2[j] for j in range(k // (i + 1))] for i in range(k // 2)]
    tail = [w1[i] * w2[0] for i in range(k // 2, k)]
    b1 = _merge_desc(rows[1] + tail[::-1])
    b2 = _sort_desc(rows[2] + rows[3] + rows[4] + rows[5] + rows[6])
    t = _top_merge(_top_merge(rows[0], b1), b2)
    t[k - 1] = jnp.maximum(t[k - 1], rows[7][0])
    t[k - 2] = jnp.maximum(t[k - 2], rows[7][1])
    return _merge_desc(t)


def _top16_of_rows(s_ref, c):
    blocks = []
    for b in range(PEER_NKEYS // PEER_TOPK):
        blocks.append(_sort_desc([s_ref[c, (b * PEER_TOPK + k) * 8:(b * PEER_TOPK + k + 1) * 8, :]
                                  for k in range(PEER_TOPK)]))
    while len(blocks) > 1:
        blocks = [_top_merge(blocks[i], blocks[i + 1]) for i in range(0, len(blocks), 2)]
    return blocks[0]


def _peer_keys_kernel(xt_ref, wqt_ref, keys_ref, a1_ref, a2_ref, tau_ref, s_even, s_odd):
    tm = xt_ref.shape[1]
    groups = tm // LANES
    bufs = (s_even, s_odd)
    if groups < 8:
        for s_ref in bufs:
            s_ref[...] = jnp.zeros_like(s_ref)

    def scores(h, slot):
        s_ref = bufs[slot]
        r0 = pl.multiple_of(h * 2 * PEER_DHALF, 2 * PEER_DHALF)
        q = jnp.dot(wqt_ref[pl.ds(r0, 2 * PEER_DHALF), :], xt_ref[...], preferred_element_type=F32).astype(BF16)
        s1 = jnp.dot(keys_ref[h, 0], q[:PEER_DHALF], preferred_element_type=F32)
        s2 = jnp.dot(keys_ref[h, 1], q[PEER_DHALF:], preferred_element_type=F32)
        a2_ref[h] = jnp.exp(s2 - jnp.max(s2, axis=0, keepdims=True))
        for g in range(groups):
            s_ref[0, pl.ds(g, PEER_NKEYS, stride=8), :] = s1[:, g * LANES:(g + 1) * LANES]
            s_ref[1, pl.ds(g, PEER_NKEYS, stride=8), :] = s2[:, g * LANES:(g + 1) * LANES]

    def select(h, slot):
        s_ref = bufs[slot]
        t1 = _top16_of_rows(s_ref, 0)
        t2 = _top16_of_rows(s_ref, 1)
        w1 = [jnp.exp(t - t1[0]) for t in t1]
        w2 = [jnp.exp(t - t2[0]) for t in t2]
        top = _top_products(w1, w2)
        z = top[0]
        for m in top[1:]:
            z = z + m
        inv_z = 1.0 / z
        tau_ref[h] = _top_products([w * inv_z for w in w1], w2)[PEER_TOPK - 1][:groups]
        for k in range(PEER_NKEYS):
            a1_ref[h, k] = (jnp.exp(s_ref[0, k * 8:(k + 1) * 8, :] - t1[0]) * inv_z)[:groups]

    def body(j, carry):
        h = 2 * j
        scores(h, 0)
        scores(h + 1, 1)
        select(h, 0)
        select(h + 1, 1)
        return carry

    lax.fori_loop(0, PEER_HEADS // 2, body, 0)


def _peer_keys(xnt, wqt, keys, tm):
    d, t = xnt.shape
    dq = wqt.shape[0]
    groups = tm // LANES
    return pl.pallas_call(
        _peer_keys_kernel,
        grid=(t // tm,),
        in_specs=[
            pl.BlockSpec((d, tm), lambda i: (0, i)),
            pl.BlockSpec((dq, d), lambda i: (0, 0)),
            pl.BlockSpec(keys.shape, lambda i: (0, 0, 0, 0)),
        ],
        out_specs=[
            pl.BlockSpec((PEER_HEADS, PEER_NKEYS, groups, LANES), lambda i: (0, 0, i, 0)),
            pl.BlockSpec((PEER_HEADS, PEER_NKEYS, tm), lambda i: (0, 0, i)),
            pl.BlockSpec((PEER_HEADS, groups, LANES), lambda i: (0, i, 0)),
        ],
        out_shape=[
            jax.ShapeDtypeStruct((PEER_HEADS, PEER_NKEYS, t // LANES, LANES), F32),
            jax.ShapeDtypeStruct((PEER_HEADS, PEER_NKEYS, t), F32),
            jax.ShapeDtypeStruct((PEER_HEADS, t // LANES, LANES), F32),
        ],
        scratch_shapes=[pltpu.VMEM((2, PEER_NKEYS * 8, LANES), F32)] * 2,
        compiler_params=_params("arbitrary"),
        name="peer_keys",
    )(xnt, wqt, keys)


def _gelu_tanh(x):
    c = math.sqrt(2.0 / math.pi)
    return x * (0.5 * (1.0 + jnp.tanh(c * (x + 0.044715 * (x * x * x)))))


def _peer_dense_kernel(xt_ref, u_ref, vt_ref, a1_ref, a2_ref, tau_ref, h_hbm, o_hbm,
                       acc, fin, a_even, a_odd, sem_in, sem_out, *, chunk):
    i, e = pl.program_id(0), pl.program_id(1)
    n_blocks, n_e = pl.num_programs(0), pl.num_programs(1) - 1
    eb, tm = a_even.shape
    keys_per_step = eb // PEER_NKEYS
    key_base = jnp.maximum(e - 1, 0) * keys_per_step

    def block_rows(blk):
        return pl.ds(pl.multiple_of(blk * tm, tm), tm)

    def load(blk):
        return pltpu.make_async_copy(h_hbm.at[block_rows(blk), :], fin, sem_in)

    def store(blk):
        return pltpu.make_async_copy(fin, o_hbm.at[block_rows(blk), :], sem_out)

    @pl.when(e == 0)
    def _():
        acc[...] = jnp.zeros_like(acc)
        a_odd[...] = jnp.zeros_like(a_odd)

    @pl.when(e == 1)
    def _():
        @pl.when(i > 0)
        def _():
            store(i - 1).wait()
        load(i).start()

    def run(a_write, a_read):
        for q in range(tm // chunk):
            lanes = slice(q * chunk, (q + 1) * chunk)
            a_write[:, lanes] = jnp.dot(u_ref[...], xt_ref[:, lanes], preferred_element_type=F32)
            tiles = []
            for k in range(keys_per_step):
                rows = slice(k * PEER_NKEYS, (k + 1) * PEER_NKEYS)
                cols = []
                for g in range(q * chunk // LANES, (q + 1) * chunk // LANES):
                    glanes = slice(g * LANES, (g + 1) * LANES)
                    gate = jnp.zeros((PEER_NKEYS, LANES), F32)
                    for h in range(PEER_HEADS):
                        p = a1_ref[h, pl.ds(key_base + k, 1), g, :] * a2_ref[h, :, glanes]
                        gate = gate + jnp.where(p >= tau_ref[h, g:g + 1, :], p, 0.0)
                    cols.append((_gelu_tanh(a_read[rows, glanes]) * gate).astype(BF16))
                tiles.append(jnp.concatenate(cols, axis=1))
            half = len(tiles) // 2
            h_lo = jnp.concatenate(tiles[:half], axis=0)
            h_hi = jnp.concatenate(tiles[half:], axis=0)
            acc[:, lanes] += (jnp.dot(vt_ref[:, :eb // 2], h_lo, preferred_element_type=F32)
                              + jnp.dot(vt_ref[:, eb // 2:], h_hi, preferred_element_type=F32))

    @pl.when(e % 2 == 0)
    def _():
        run(a_even, a_odd)

    @pl.when(e % 2 == 1)
    def _():
        run(a_odd, a_even)

    @pl.when(e == n_e)
    def _():
        load(i).wait()
        for q in range(tm // chunk):
            lanes = slice(q * chunk, (q + 1) * chunk)
            fin[lanes, :] += jnp.transpose(acc[:, lanes])
        store(i).start()

        @pl.when(i == n_blocks - 1)
        def _():
            store(i).wait()


def _peer_dense(xnt, h2, u, vt, a1, a2, tau, tm, eb):
    d, t = xnt.shape
    n_e = u.shape[0] // eb
    groups = tm // LANES
    return pl.pallas_call(
        functools.partial(_peer_dense_kernel, chunk=min(tm, 256)),
        grid=(t // tm, n_e + 1),
        in_specs=[
            pl.BlockSpec((d, tm), lambda i, e: (0, i)),
            pl.BlockSpec((eb, d), lambda i, e: (jnp.minimum(e, n_e - 1), 0)),
            pl.BlockSpec((d, eb), lambda i, e: (0, jnp.maximum(e - 1, 0))),
            pl.BlockSpec((PEER_HEADS, PEER_NKEYS, groups, LANES), lambda i, e: (0, 0, i, 0)),
            pl.BlockSpec((PEER_HEADS, PEER_NKEYS, tm), lambda i, e: (0, 0, i)),
            pl.BlockSpec((PEER_HEADS, groups, LANES), lambda i, e: (0, i, 0)),
            pl.BlockSpec(memory_space=pl.ANY),
        ],
        out_specs=pl.BlockSpec(memory_space=pl.ANY),
        out_shape=jax.ShapeDtypeStruct((t, d), F32),
        scratch_shapes=[pltpu.VMEM((d, tm), F32), pltpu.VMEM((tm, d), F32),
                        pltpu.VMEM((eb, tm), F32), pltpu.VMEM((eb, tm), F32),
                        pltpu.SemaphoreType.DMA(()), pltpu.SemaphoreType.DMA(())],
        compiler_params=_params("arbitrary", "arbitrary"),
        name="peer_dense",
    )(xnt, u, vt, a1, a2, tau, h2)


def _tile(n, want):
    t = min(n, want)
    assert n % t == 0, (n, want)
    return t


def kernel(x, meta_tokens, hgrn_lb_logits, ln_mix_g, w_in, q_norm_g, k_norm_g, attn_sinks, hgrn_norm_g,
           w_out, ln_ffn_g, peer_w_q, peer_sub_keys, peer_u, peer_v):
    batch, seq, d = x.shape
    assert w_in.shape[0] == 1 and hgrn_lb_logits.shape[0] == 2, "single-layer trunk"
    assert seq % ATT_BLOCK == 0 and peer_sub_keys.shape[1:] == (PEER_HEADS, 2, PEER_NKEYS, PEER_DHALF)
    t = batch * seq
    x2d = x.reshape(t, d)

    wi = w_in[0]
    o_q, o_k, o_v, o_hq = 0, Q_COLS, Q_COLS + KV_COLS, Q_COLS + 2 * KV_COLS

    def rope_order(c0, heads):
        runs = [(0, ROPE_HALF), (ROPE_DIM, ROPE_DIM + HEAD_DIM // 2 - ROPE_HALF), (ROPE_HALF, ROPE_DIM),
                (ROPE_DIM + HEAD_DIM // 2 - ROPE_HALF, HEAD_DIM)]
        return [wi[:, c0 + h * HEAD_DIM + a:c0 + h * HEAD_DIM + b] for h in range(heads) for a, b in runs]

    pieces = rope_order(o_q, ATT_Q_HEADS) + [wi[:, o_hq:]] + rope_order(o_k, ATT_KV_HEADS) + [wi[:, o_v:o_hq]]
    w_perm = jnp.concatenate([p.astype(BF16) for p in pieces], axis=1)
    col_q, col_hq, col_k, col_v = 0, Q_COLS, Q_COLS + 4 * HK, Q_COLS + 4 * HK + KV_COLS

    proj = _in_proj(x2d, ln_mix_g[0], w_perm, _tile(t, 1024), IN_PROJ_TN)
    proj_meta = _in_proj(meta_tokens.astype(F32), ln_mix_g[0], w_perm, N_META, IN_PROJ_TN)

    att = _attention(proj, proj_meta, attn_sinks[0].astype(F32), q_norm_g[0], k_norm_g[0],
                     batch, seq, col_q, col_k, col_v)
    rec = _hgrn(proj, proj_meta, hgrn_lb_logits.astype(F32), hgrn_norm_g[0], batch, seq, col_hq,
                _tile(seq, 256))

    wo = w_out[0].astype(BF16)
    h2, xnt = _out_proj(x2d, att, rec, wo[:Q_COLS], wo[Q_COLS:], ln_ffn_g[0], _tile(t, 256))

    wqt = jnp.transpose(peer_w_q[0]).astype(BF16)
    keys = peer_sub_keys[0].astype(BF16)
    a1, a2, tau = _peer_keys(xnt, wqt, keys, _tile(t, 1024))

    vt = jnp.transpose(peer_v[0]).astype(BF16)
    out = _peer_dense(xnt, h2, peer_u[0].astype(BF16), vt, a1, a2, tau, _tile(t, 1024), 512)
    return out.reshape(batch, seq, d)
```

```python
import functools
import math

import jax
import jax.numpy as jnp
import numpy as np
from jax import lax
from jax.experimental import pallas as pl
from jax.experimental.pallas import tpu as pltpu

F32 = jnp.float32
BF16 = jnp.bfloat16

N_META = 16
RMS_EPS = 1e-6
HEAD_DIM = 128
ATT_Q_HEADS = 8
ATT_KV_HEADS = 2
ATT_GROUP = ATT_Q_HEADS // ATT_KV_HEADS
ATT_BLOCK = 128
ROPE_THETA = 500000.0
ROPE_DIM = HEAD_DIM // 4
HGRN_HEADS = 8
HGRN_DK = 128
HGRN_CHUNK = 64
Q_COLS = ATT_Q_HEADS * HEAD_DIM
KV_COLS = ATT_KV_HEADS * HEAD_DIM
HK = HGRN_HEADS * HGRN_DK
IN_COLS = Q_COLS + 2 * KV_COLS + 4 * HK
IN_PROJ_TM = 1024
IN_PROJ_TN = IN_COLS // 4
HGRN_ROWS_PER_STEP = 256
OUT_PROJ_TM = 512
PEER_TM = 1024
PEER_EXPERT_BLOCK = 512
PEER_HEADS = 8
PEER_NKEYS = 128
PEER_DHALF = 128
PEER_TOPK = 16

V7X_VMEM_BYTES = 64 * 1024 * 1024
VMEM_LIMIT = V7X_VMEM_BYTES - 8 * 1024 * 1024
LANES = 128
HGRN_EXP_CLAMP = 85.0
MASKED = -1e30


def _params(*sem):
    return pltpu.CompilerParams(dimension_semantics=sem, vmem_limit_bytes=VMEM_LIMIT)


def _rms(x, g):
    ms = jnp.mean(x * x, axis=-1, keepdims=True)
    return x * lax.rsqrt(ms + RMS_EPS) * g


def _sigmoid(x):
    return 1.0 / (1.0 + jnp.exp(-x))


def _in_proj_kernel(x_ref, g_ref, w_ref, o_ref, xn_ref):
    @pl.when(pl.program_id(1) == 0)
    def _():
        xn_ref[...] = _rms(x_ref[...], g_ref[...]).astype(BF16)

    o_ref[...] = jnp.dot(xn_ref[...], w_ref[...], preferred_element_type=F32)


def _in_proj(x2d, g, w, tm, tn):
    m, d = x2d.shape
    n = w.shape[1]
    return pl.pallas_call(
        _in_proj_kernel,
        grid=(m // tm, n // tn),
        in_specs=[
            pl.BlockSpec((tm, d), lambda i, j: (i, 0)),
            pl.BlockSpec((1, d), lambda i, j: (0, 0)),
            pl.BlockSpec((d, tn), lambda i, j: (0, j)),
        ],
        out_specs=pl.BlockSpec((tm, tn), lambda i, j: (i, j)),
        out_shape=jax.ShapeDtypeStruct((m, n), F32),
        scratch_shapes=[pltpu.VMEM((tm, d), BF16)],
        compiler_params=_params("arbitrary", "arbitrary"),
        name="in_proj",
    )(x2d, g.reshape(1, d), w)


ROPE_HALF = ROPE_DIM // 2
ROPE_LANE_ORDER = np.concatenate([
    np.arange(0, ROPE_HALF), np.arange(ROPE_DIM, ROPE_DIM + HEAD_DIM // 2 - ROPE_HALF),
    np.arange(ROPE_HALF, ROPE_DIM), np.arange(ROPE_DIM + HEAD_DIM // 2 - ROPE_HALF, HEAD_DIM)])


def _rope(x, tab_ref):
    return x * tab_ref[0] + pltpu.roll(x, HEAD_DIM // 2, 1) * tab_ref[1]


def _attn_kernel(sink_ref, q_ref, k_ref, v_ref, km_ref, vm_ref, qg_ref, kg_ref, tab_ref, tabm_ref, o_ref,
                 kprev_s, vprev_s, kmeta_s, vmeta_s):
    b, n = pl.program_id(0), pl.program_id(1)
    blk = ATT_BLOCK
    rows = ATT_GROUP * blk
    scale = HEAD_DIM ** -0.5

    @pl.when(jnp.logical_and(b == 0, n == 0))
    def _():
        pad = jnp.zeros((blk - N_META, HEAD_DIM), BF16)
        for g in range(ATT_KV_HEADS):
            ksl = slice(g * HEAD_DIM, (g + 1) * HEAD_DIM)
            k_meta = _rope(_rms(km_ref[:, ksl], kg_ref[...]), tabm_ref).astype(BF16)
            kmeta_s[g] = jnp.concatenate([k_meta, pad], axis=0)
            vmeta_s[g] = jnp.concatenate([vm_ref[:, ksl].astype(BF16), pad], axis=0)

    @pl.when(n == 0)
    def _():
        kprev_s[...] = jnp.zeros_like(kprev_s)
        vprev_s[...] = jnp.zeros_like(vprev_s)

    qi = lax.broadcasted_iota(jnp.int32, (rows, blk), 0) & (blk - 1)
    ki = lax.broadcasted_iota(jnp.int32, (rows, blk), 1)
    own_mask = ki <= qi
    prev_mask = ki > jnp.where(n > 0, qi, blk)
    meta_mask = ki < N_META
    head_row = lax.shift_right_logical(lax.broadcasted_iota(jnp.int32, (rows, 1), 0), 7)
    nt = (((1,), (1,)), ((), ()))
    for g in range(ATT_KV_HEADS):
        ksl = slice(g * HEAD_DIM, (g + 1) * HEAD_DIM)
        k_own = _rope(_rms(k_ref[:, ksl], kg_ref[...]), tab_ref).astype(BF16)
        v_own = v_ref[:, ksl].astype(BF16)
        keys = jnp.concatenate([kprev_s[g], k_own, kmeta_s[g]], axis=0)
        vals = jnp.concatenate([vprev_s[g], v_own, vmeta_s[g]], axis=0)
        kprev_s[g] = k_own
        vprev_s[g] = v_own
        qs = []
        sink = jnp.zeros((rows, 1), F32)
        for hh in range(ATT_GROUP):
            h = g * ATT_GROUP + hh
            qs.append(_rope(_rms(q_ref[:, h * HEAD_DIM:(h + 1) * HEAD_DIM], qg_ref[...]), tab_ref))
            sink = jnp.where(head_row == hh, sink_ref[h], sink)
        q = jnp.concatenate(qs, axis=0).astype(BF16)
        s = lax.dot_general(q, keys, nt, preferred_element_type=F32) * scale
        s_prev = jnp.where(prev_mask, s[:, :blk], MASKED)
        s_own = jnp.where(own_mask, s[:, blk:2 * blk], MASKED)
        s_meta = jnp.where(meta_mask, s[:, 2 * blk:], MASKED)
        m = jnp.maximum(jnp.max(jnp.maximum(jnp.maximum(s_prev, s_own), s_meta), axis=-1, keepdims=True), sink)
        p_prev, p_own, p_meta = jnp.exp(s_prev - m), jnp.exp(s_own - m), jnp.exp(s_meta - m)
        denom = jnp.sum(p_prev + p_own + p_meta, axis=-1, keepdims=True) + jnp.exp(sink - m)
        p = jnp.concatenate([p_prev, p_own, p_meta], axis=1).astype(BF16)
        out = jnp.dot(p, vals, preferred_element_type=F32) / denom
        for hh in range(ATT_GROUP):
            h = g * ATT_GROUP + hh
            o_ref[:, h * HEAD_DIM:(h + 1) * HEAD_DIM] = out[hh * blk:(hh + 1) * blk].astype(BF16)


def _rope_tables(pos):
    inv = ROPE_THETA ** (-jnp.arange(0, ROPE_DIM, 2, dtype=F32) / ROPE_DIM)
    ang = pos.astype(F32)[:, None] * inv[None, :]
    cos, sin = jnp.cos(ang), jnp.sin(ang)
    n = pos.shape[0]
    gap = HEAD_DIM // 2 - ROPE_HALF
    c = jnp.concatenate([cos, jnp.ones((n, gap), F32), cos, jnp.ones((n, gap), F32)], axis=1)
    s = jnp.concatenate([-sin, jnp.zeros((n, gap), F32), sin, jnp.zeros((n, gap), F32)], axis=1)
    return jnp.stack([c, s])


def _attention(proj, proj_meta, sinks, qg, kg, batch, seq, col_q, col_k, col_v):
    nb = seq // ATT_BLOCK
    t = batch * seq
    tab = _rope_tables(jnp.arange(seq) + N_META)
    tabm = _rope_tables(jnp.arange(N_META))
    kb, vb = col_k // KV_COLS, col_v // KV_COLS
    kv_scratch = pltpu.VMEM((ATT_KV_HEADS, ATT_BLOCK, HEAD_DIM), BF16)
    return pl.pallas_call(
        _attn_kernel,
        grid=(batch, nb),
        in_specs=[
            pl.BlockSpec(memory_space=pltpu.SMEM),
            pl.BlockSpec((ATT_BLOCK, Q_COLS), lambda b, n: (b * nb + n, col_q // Q_COLS)),
            pl.BlockSpec((ATT_BLOCK, KV_COLS), lambda b, n: (b * nb + n, kb)),
            pl.BlockSpec((ATT_BLOCK, KV_COLS), lambda b, n: (b * nb + n, vb)),
            pl.BlockSpec((N_META, KV_COLS), lambda b, n: (0, kb)),
            pl.BlockSpec((N_META, KV_COLS), lambda b, n: (0, vb)),
            pl.BlockSpec((1, HEAD_DIM), lambda b, n: (0, 0)),
            pl.BlockSpec((1, HEAD_DIM), lambda b, n: (0, 0)),
            pl.BlockSpec((2, ATT_BLOCK, HEAD_DIM), lambda b, n: (0, n, 0)),
            pl.BlockSpec((2, N_META, HEAD_DIM), lambda b, n: (0, 0, 0)),
        ],
        out_specs=pl.BlockSpec((ATT_BLOCK, Q_COLS), lambda b, n: (b * nb + n, 0)),
        out_shape=jax.ShapeDtypeStruct((t, Q_COLS), BF16),
        scratch_shapes=[kv_scratch, kv_scratch, kv_scratch, kv_scratch],
        compiler_params=_params("arbitrary", "arbitrary"),
        name="swa_attention",
    )(sinks, proj, proj, proj, proj_meta, proj_meta,
      qg[ROPE_LANE_ORDER].reshape(1, HEAD_DIM), kg[ROPE_LANE_ORDER].reshape(1, HEAD_DIM), tab, tabm)


def _cumsum_rows(x):
    n = x.shape[0]
    row = lax.broadcasted_iota(jnp.int32, x.shape, 0)
    shift = 1
    while shift < n:
        x = x + jnp.where(row >= shift, pltpu.roll(x, shift, 0), 0.0)
        shift *= 2
    return x


def _hgrn_gates(hq, hf, lb):
    f = lb + (1.0 - lb) * _sigmoid(hf)
    return hq * _sigmoid(hq), (1.0 - lb) * _sigmoid(-hf), _cumsum_rows(jnp.log(f))


def _hgrn_needs_direct(hf):
    half = hf.shape[0] // 2
    steep = jnp.maximum(-hf, 0.0)
    bound = half * math.log(2.0) + jnp.maximum(jnp.sum(steep[:half], axis=0, keepdims=True),
                                               jnp.sum(steep[half:], axis=0, keepdims=True))
    return (jnp.max(bound) > HGRN_EXP_CLAMP).astype(jnp.int32)


def _hgrn_direct_scores(hq, hf, lb, sc_ref, bs_ref, ks_ref):
    c = hq.shape[0]
    q, kk, b = _hgrn_gates(hq, hf, lb)
    bs_ref[...] = b
    ks_ref[...] = kk
    col_id = lax.broadcasted_iota(jnp.int32, (c, c), 1)

    def column(s, accs):
        w = q * jnp.exp(jnp.minimum(b - bs_ref[pl.ds(s, 1), :], 0.0)) * ks_ref[pl.ds(s, 1), :]
        return tuple(
            jnp.where(col_id == s, jnp.sum(w[:, h * HGRN_DK:(h + 1) * HGRN_DK], axis=1, keepdims=True), acc)
            for h, acc in enumerate(accs))

    accs = lax.fori_loop(0, c, column, tuple(jnp.zeros((c, c), F32) for _ in range(HGRN_HEADS)))
    for h in range(HGRN_HEADS):
        sc_ref[h] = accs[h]


def _hgrn_chunk(hq, hf, hi, hg, lb, ng, st_ref, sc_ref, direct):
    c = hq.shape[0]
    half = c // 2
    direct = direct == 1
    q, kk, b = _hgrn_gates(hq, hf, lb)
    b_last = b[c - 1:c, :]
    b_mid = b[half - 1:half, :]
    q_mid = (q * jnp.exp(jnp.minimum(b - b_mid, HGRN_EXP_CLAMP))).astype(BF16)
    k_mid = (kk * jnp.exp(jnp.minimum(b_mid - b, HGRN_EXP_CLAMP))).astype(BF16)
    q_in = (q * jnp.exp(b)).astype(BF16)
    k_out = (kk * jnp.exp(b_last - b)).astype(BF16)
    decay = jnp.exp(b_last)
    v = hi.astype(BF16)
    causal = (lax.broadcasted_iota(jnp.int32, (c, c), 0) >= lax.broadcasted_iota(jnp.int32, (c, c), 1))
    nt = (((1,), (1,)), ((), ()))
    tn = (((0,), (0,)), ((), ()))
    outs = []
    for h in range(HGRN_HEADS):
        sl = slice(h * HGRN_DK, (h + 1) * HGRN_DK)
        st = st_ref[h]
        scores = lax.dot_general(q_mid[:, sl], k_mid[:, sl], nt, preferred_element_type=F32)
        scores = jnp.where(causal, jnp.where(direct, sc_ref[h], scores), 0.0).astype(BF16)
        o = (jnp.dot(scores, v[:, sl], preferred_element_type=F32)
             + lax.dot_general(q_in[:, sl], st.astype(BF16), nt, preferred_element_type=F32))
        st_ref[h] = (st * decay[:, sl]
                     + lax.dot_general(v[:, sl], k_out[:, sl], tn, preferred_element_type=F32))
        outs.append(_rms(o, ng[:, sl]))
    o_all = jnp.concatenate(outs, axis=1)
    return o_all * (hg * _sigmoid(hg))


def _lower_bound(lbl_ref):
    l0, l1 = lbl_ref[0:1, :], lbl_ref[1:2, :]
    m = jnp.maximum(l0, l1)
    e0, e1 = jnp.exp(l0 - m), jnp.exp(l1 - m)
    return e0 / (e0 + e1)


def _hgrn_meta_kernel(hq_ref, hf_ref, hi_ref, hg_ref, lbl_ref, ng_ref, st_out_ref, sc_ref, bs_ref, ks_ref):
    st_out_ref[...] = jnp.zeros_like(st_out_ref)
    sc_ref[...] = jnp.zeros_like(sc_ref)
    lb = _lower_bound(lbl_ref)
    direct = _hgrn_needs_direct(hf_ref[...])

    @pl.when(direct == 1)
    def _():
        _hgrn_direct_scores(hq_ref[...], hf_ref[...], lb, sc_ref, bs_ref, ks_ref)

    _hgrn_chunk(hq_ref[...], hf_ref[...], hi_ref[...], hg_ref[...], lb, ng_ref[...], st_out_ref, sc_ref, direct)


def _hgrn_work_buffers(c):
    return [pltpu.VMEM((HGRN_HEADS, c, c), F32), pltpu.VMEM((c, HK), F32), pltpu.VMEM((c, HK), F32)]


def _hgrn_kernel(hq_ref, hf_ref, hi_ref, hg_ref, lbl_ref, ng_ref, st0_ref, o_ref, st_ref, sc_ref, bs_ref, ks_ref,
                 *, chunks):
    @pl.when(pl.program_id(1) == 0)
    def _():
        st_ref[...] = st0_ref[...]
        sc_ref[...] = jnp.zeros_like(sc_ref)

    lb = _lower_bound(lbl_ref)

    def chunk_rows(ci):
        return pl.ds(pl.multiple_of(ci * HGRN_CHUNK, HGRN_CHUNK), HGRN_CHUNK)

    def body(ci, direct):
        rows = chunk_rows(ci)

        @pl.when(direct == 1)
        def _():
            _hgrn_direct_scores(hq_ref[rows, :], hf_ref[rows, :], lb, sc_ref, bs_ref, ks_ref)

        o = _hgrn_chunk(hq_ref[rows, :], hf_ref[rows, :], hi_ref[rows, :], hg_ref[rows, :], lb, ng_ref[...],
                        st_ref, sc_ref, direct)
        o_ref[rows, :] = o.astype(BF16)
        return _hgrn_needs_direct(hf_ref[chunk_rows(jnp.minimum(ci + 1, chunks - 1)), :])

    lax.fori_loop(0, chunks, body, _hgrn_needs_direct(hf_ref[0:HGRN_CHUNK, :]))


def _hgrn(proj, proj_meta, lb_logits, norm_g, batch, seq, col_hq, rows_per_step):
    t = batch * seq
    cb = col_hq // HK
    state_shape = (HGRN_HEADS, HGRN_DK, HGRN_DK)
    vec = pl.BlockSpec((1, HK), lambda *_: (0, 0))
    lbs = pl.BlockSpec((2, HK), lambda *_: (0, 0))
    st_meta = pl.pallas_call(
        _hgrn_meta_kernel,
        grid=(1,),
        in_specs=[pl.BlockSpec((N_META, HK), lambda i, k=k: (0, cb + k)) for k in range(4)] + [lbs, vec],
        out_specs=pl.BlockSpec(state_shape, lambda i: (0, 0, 0)),
        out_shape=jax.ShapeDtypeStruct(state_shape, F32),
        scratch_shapes=_hgrn_work_buffers(N_META),
        compiler_params=_params("arbitrary"),
        name="hgrn2_meta",
    )(proj_meta, proj_meta, proj_meta, proj_meta, lb_logits, norm_g.reshape(1, HK))
    steps = seq // rows_per_step
    return pl.pallas_call(
        functools.partial(_hgrn_kernel, chunks=rows_per_step // HGRN_CHUNK),
        grid=(batch, steps),
        in_specs=[pl.BlockSpec((rows_per_step, HK), lambda b, s, k=k: (b * steps + s, cb + k)) for k in range(4)]
        + [lbs, vec, pl.BlockSpec(state_shape, lambda b, s: (0, 0, 0))],
        out_specs=pl.BlockSpec((rows_per_step, HK), lambda b, s: (b * steps + s, 0)),
        out_shape=jax.ShapeDtypeStruct((t, HK), BF16),
        scratch_shapes=[pltpu.VMEM(state_shape, F32)] + _hgrn_work_buffers(HGRN_CHUNK),
        compiler_params=_params("arbitrary", "arbitrary"),
        name="hgrn2",
    )(proj, proj, proj, proj, lb_logits, norm_g.reshape(1, HK), st_meta)


def _out_proj_kernel(x_ref, att_ref, rec_ref, wa_ref, wr_ref, g_ref, h_ref, xnt_ref):
    h = (x_ref[...]
         + jnp.dot(att_ref[...], wa_ref[...], preferred_element_type=F32)
         + jnp.dot(rec_ref[...], wr_ref[...], preferred_element_type=F32))
    h_ref[...] = h
    xnt_ref[...] = jnp.transpose(_rms(h, g_ref[...])).astype(BF16)


def _out_proj(x2d, att, rec, wa, wr, g, tm):
    t, d = x2d.shape
    return pl.pallas_call(
        _out_proj_kernel,
        grid=(t // tm,),
        in_specs=[
            pl.BlockSpec((tm, d), lambda i: (i, 0)),
            pl.BlockSpec((tm, att.shape[1]), lambda i: (i, 0)),
            pl.BlockSpec((tm, rec.shape[1]), lambda i: (i, 0)),
            pl.BlockSpec(wa.shape, lambda i: (0, 0)),
            pl.BlockSpec(wr.shape, lambda i: (0, 0)),
            pl.BlockSpec((1, d), lambda i: (0, 0)),
        ],
        out_specs=[pl.BlockSpec((tm, d), lambda i: (i, 0)), pl.BlockSpec((d, tm), lambda i: (0, i))],
        out_shape=[jax.ShapeDtypeStruct((t, d), F32), jax.ShapeDtypeStruct((d, t), BF16)],
        compiler_params=_params("arbitrary"),
        name="out_proj",
    )(x2d, att, rec, wa, wr, g.reshape(1, d))


def _sort_desc(v):
    v = list(v)
    n = len(v)
    k = 2
    while k <= n:
        j = k // 2
        while j >= 1:
            for i in range(n):
                l = i ^ j
                if l > i:
                    hi, lo = jnp.maximum(v[i], v[l]), jnp.minimum(v[i], v[l])
                    v[i], v[l] = (hi, lo) if (i & k) == 0 else (lo, hi)
            j //= 2
        k *= 2
    return v


def _merge_desc(v):
    v = list(v)
    j = len(v) // 2
    while j >= 1:
        for i in range(len(v)):
            l = i ^ j
            if l > i:
                v[i], v[l] = jnp.maximum(v[i], v[l]), jnp.minimum(v[i], v[l])
        j //= 2
    return v


def _top_merge(a, b):
    n = len(a)
    return _merge_desc([jnp.maximum(a[i], b[n - 1 - i]) for i in range(n)])


def _top_products(w1, w2):
    k = PEER_TOPK
    rows = [[w1[i] * w2[j] for j in range(k // (i + 1))] for i in range(k // 2)]
    tail = [w1[i] * w2[0] for i in range(k // 2, k)]
    b1 = _merge_desc(rows[1] + tail[::-1])
    b2 = _sort_desc(rows[2] + rows[3] + rows[4] + rows[5] + rows[6])
    t = _top_merge(_top_merge(rows[0], b1), b2)
    t[k - 1] = jnp.maximum(t[k - 1], rows[7][0])
    t[k - 2] = jnp.maximum(t[k - 2], rows[7][1])
    return _merge_desc(t)


def _top16_of_rows(s_ref, c):
    blocks = []
    for b in range(PEER_NKEYS // PEER_TOPK):
        blocks.append(_sort_desc([s_ref[c, (b * PEER_TOPK + k) * 8:(b * PEER_TOPK + k + 1) * 8, :]
                                  for k in range(PEER_TOPK)]))
    while len(blocks) > 1:
        blocks = [_top_merge(blocks[i], blocks[i + 1]) for i in range(0, len(blocks), 2)]
    return blocks[0]


def _peer_keys_kernel(xt_ref, wqt_ref, keys_ref, a1_ref, a2_ref, tau_ref, s_even, s_odd):
    tm = xt_ref.shape[1]
    groups = tm // LANES
    bufs = (s_even, s_odd)
    if groups < 8:
        for s_ref in bufs:
            s_ref[...] = jnp.zeros_like(s_ref)

    def scores(h, slot):
        s_ref = bufs[slot]
        r0 = pl.multiple_of(h * 2 * PEER_DHALF, 2 * PEER_DHALF)
        q = jnp.dot(wqt_ref[pl.ds(r0, 2 * PEER_DHALF), :], xt_ref[...], preferred_element_type=F32).astype(BF16)
        s1 = jnp.dot(keys_ref[h, 0], q[:PEER_DHALF], preferred_element_type=F32)
        s2 = jnp.dot(keys_ref[h, 1], q[PEER_DHALF:], preferred_element_type=F32)
        a2_ref[h] = jnp.exp(s2 - jnp.max(s2, axis=0, keepdims=True))
        for g in range(groups):
            s_ref[0, pl.ds(g, PEER_NKEYS, stride=8), :] = s1[:, g * LANES:(g + 1) * LANES]
            s_ref[1, pl.ds(g, PEER_NKEYS, stride=8), :] = s2[:, g * LANES:(g + 1) * LANES]

    def select(h, slot):
        s_ref = bufs[slot]
        t1 = _top16_of_rows(s_ref, 0)
        t2 = _top16_of_rows(s_ref, 1)
        w1 = [jnp.exp(t - t1[0]) for t in t1]
        w2 = [jnp.exp(t - t2[0]) for t in t2]
        top = _top_products(w1, w2)
        z = top[0]
        for m in top[1:]:
            z = z + m
        inv_z = 1.0 / z
        tau_ref[h] = _top_products([w * inv_z for w in w1], w2)[PEER_TOPK - 1][:groups]
        for k in range(PEER_NKEYS):
            a1_ref[h, k] = (jnp.exp(s_ref[0, k * 8:(k + 1) * 8, :] - t1[0]) * inv_z)[:groups]

    def body(j, carry):
        h = 2 * j
        scores(h, 0)
        scores(h + 1, 1)
        select(h, 0)
        select(h + 1, 1)
        return carry

    lax.fori_loop(0, PEER_HEADS // 2, body, 0)


def _peer_keys(xnt, wqt, keys, tm):
    d, t = xnt.shape
    dq = wqt.shape[0]
    groups = tm // LANES
    return pl.pallas_call(
        _peer_keys_kernel,
        grid=(t // tm,),
        in_specs=[
            pl.BlockSpec((d, tm), lambda i: (0, i)),
            pl.BlockSpec((dq, d), lambda i: (0, 0)),
            pl.BlockSpec(keys.shape, lambda i: (0, 0, 0, 0)),
        ],
        out_specs=[
            pl.BlockSpec((PEER_HEADS, PEER_NKEYS, groups, LANES), lambda i: (0, 0, i, 0)),
            pl.BlockSpec((PEER_HEADS, PEER_NKEYS, tm), lambda i: (0, 0, i)),
            pl.BlockSpec((PEER_HEADS, groups, LANES), lambda i: (0, i, 0)),
        ],
        out_shape=[
            jax.ShapeDtypeStruct((PEER_HEADS, PEER_NKEYS, t // LANES, LANES), F32),
            jax.ShapeDtypeStruct((PEER_HEADS, PEER_NKEYS, t), F32),
            jax.ShapeDtypeStruct((PEER_HEADS, t // LANES, LANES), F32),
        ],
        scratch_shapes=[pltpu.VMEM((2, PEER_NKEYS * 8, LANES), F32)] * 2,
        compiler_params=_params("arbitrary"),
        name="peer_keys",
    )(xnt, wqt, keys)


def _gelu_tanh(x):
    c = math.sqrt(2.0 / math.pi)
    return x * (0.5 * (1.0 + jnp.tanh(c * (x + 0.044715 * (x * x * x)))))


def _peer_dense_kernel(xt_ref, u_ref, vt_ref, a1_ref, a2_ref, tau_ref, h_hbm, o_hbm,
                       acc, fin, a_even, a_odd, sem_in, sem_out, *, chunk):
    i, e = pl.program_id(0), pl.program_id(1)
    n_blocks, n_e = pl.num_programs(0), pl.num_programs(1) - 1
    eb, tm = a_even.shape
    keys_per_step = eb // PEER_NKEYS
    key_base = jnp.maximum(e - 1, 0) * keys_per_step

    def block_rows(blk):
        return pl.ds(pl.multiple_of(blk * tm, tm), tm)

    def load(blk):
        return pltpu.make_async_copy(h_hbm.at[block_rows(blk), :], fin, sem_in)

    def store(blk):
        return pltpu.make_async_copy(fin, o_hbm.at[block_rows(blk), :], sem_out)

    @pl.when(e == 0)
    def _():
        acc[...] = jnp.zeros_like(acc)
        a_odd[...] = jnp.zeros_like(a_odd)

    @pl.when(e == 1)
    def _():
        @pl.when(i > 0)
        def _():
            store(i - 1).wait()
        load(i).start()

    def run(a_write, a_read):
        for q in range(tm // chunk):
            lanes = slice(q * chunk, (q + 1) * chunk)
            a_write[:, lanes] = jnp.dot(u_ref[...], xt_ref[:, lanes], preferred_element_type=F32)
            tiles = []
            for k in range(keys_per_step):
                rows = slice(k * PEER_NKEYS, (k + 1) * PEER_NKEYS)
                cols = []
                for g in range(q * chunk // LANES, (q + 1) * chunk // LANES):
                    glanes = slice(g * LANES, (g + 1) * LANES)
                    gate = jnp.zeros((PEER_NKEYS, LANES), F32)
                    for h in range(PEER_HEADS):
                        p = a1_ref[h, pl.ds(key_base + k, 1), g, :] * a2_ref[h, :, glanes]
                        gate = gate + jnp.where(p >= tau_ref[h, g:g + 1, :], p, 0.0)
                    cols.append((_gelu_tanh(a_read[rows, glanes]) * gate).astype(BF16))
                tiles.append(jnp.concatenate(cols, axis=1))
            half = len(tiles) // 2
            h_lo = jnp.concatenate(tiles[:half], axis=0)
            h_hi = jnp.concatenate(tiles[half:], axis=0)
            acc[:, lanes] += (jnp.dot(vt_ref[:, :eb // 2], h_lo, preferred_element_type=F32)
                              + jnp.dot(vt_ref[:, eb // 2:], h_hi, preferred_element_type=F32))

    @pl.when(e % 2 == 0)
    def _():
        run(a_even, a_odd)

    @pl.when(e % 2 == 1)
    def _():
        run(a_odd, a_even)

    @pl.when(e == n_e)
    def _():
        load(i).wait()
        for q in range(tm // chunk):
            lanes = slice(q * chunk, (q + 1) * chunk)
            fin[lanes, :] += jnp.transpose(acc[:, lanes])
        store(i).start()

        @pl.when(i == n_blocks - 1)
        def _():
            store(i).wait()


def _peer_dense(xnt, h2, u, vt, a1, a2, tau, tm, eb):
    d, t = xnt.shape
    n_e = u.shape[0] // eb
    groups = tm // LANES
    return pl.pallas_call(
        functools.partial(_peer_dense_kernel, chunk=min(tm, 256)),
        grid=(t // tm, n_e + 1),
        in_specs=[
            pl.BlockSpec((d, tm), lambda i, e: (0, i)),
            pl.BlockSpec((eb, d), lambda i, e: (jnp.minimum(e, n_e - 1), 0)),
            pl.BlockSpec((d, eb), lambda i, e: (0, jnp.maximum(e - 1, 0))),
            pl.BlockSpec((PEER_HEADS, PEER_NKEYS, groups, LANES), lambda i, e: (0, 0, i, 0)),
            pl.BlockSpec((PEER_HEADS, PEER_NKEYS, tm), lambda i, e: (0, 0, i)),
            pl.BlockSpec((PEER_HEADS, groups, LANES), lambda i, e: (0, i, 0)),
            pl.BlockSpec(memory_space=pl.ANY),
        ],
        out_specs=pl.BlockSpec(memory_space=pl.ANY),
        out_shape=jax.ShapeDtypeStruct((t, d), F32),
        scratch_shapes=[pltpu.VMEM((d, tm), F32), pltpu.VMEM((tm, d), F32),
                        pltpu.VMEM((eb, tm), F32), pltpu.VMEM((eb, tm), F32),
                        pltpu.SemaphoreType.DMA(()), pltpu.SemaphoreType.DMA(())],
        compiler_params=_params("arbitrary", "arbitrary"),
        name="peer_dense",
    )(xnt, u, vt, a1, a2, tau, h2)


def _tile(n, want):
    t = min(n, want)
    assert n % t == 0, (n, want)
    return t


def kernel(x, meta_tokens, hgrn_lb_logits, ln_mix_g, w_in, q_norm_g, k_norm_g, attn_sinks, hgrn_norm_g,
           w_out, ln_ffn_g, peer_w_q, peer_sub_keys, peer_u, peer_v):
    batch, seq, d = x.shape
    assert w_in.shape[0] == 1 and hgrn_lb_logits.shape[0] == 2, "single-layer trunk"
    assert seq % ATT_BLOCK == 0 and peer_sub_keys.shape[1:] == (PEER_HEADS, 2, PEER_NKEYS, PEER_DHALF)
    t = batch * seq
    x2d = x.reshape(t, d)

    wi = lax.optimization_barrier(w_in[0].astype(BF16))
    o_q, o_k, o_v, o_hq = 0, Q_COLS, Q_COLS + KV_COLS, Q_COLS + 2 * KV_COLS

    def rope_order(c0, heads):
        runs = [(0, ROPE_HALF), (ROPE_DIM, ROPE_DIM + HEAD_DIM // 2 - ROPE_HALF), (ROPE_HALF, ROPE_DIM),
                (ROPE_DIM + HEAD_DIM // 2 - ROPE_HALF, HEAD_DIM)]
        return [wi[:, c0 + h * HEAD_DIM + a:c0 + h * HEAD_DIM + b] for h in range(heads) for a, b in runs]

    pieces = rope_order(o_q, ATT_Q_HEADS) + [wi[:, o_hq:]] + rope_order(o_k, ATT_KV_HEADS) + [wi[:, o_v:o_hq]]
    w_perm = jnp.concatenate(pieces, axis=1)
    col_q, col_hq, col_k, col_v = 0, Q_COLS, Q_COLS + 4 * HK, Q_COLS + 4 * HK + KV_COLS

    proj = _in_proj(x2d, ln_mix_g[0], w_perm, _tile(t, IN_PROJ_TM), IN_PROJ_TN)
    proj_meta = _in_proj(meta_tokens.astype(F32), ln_mix_g[0], w_perm, N_META, IN_PROJ_TN)

    att = _attention(proj, proj_meta, attn_sinks[0].astype(F32), q_norm_g[0], k_norm_g[0],
                     batch, seq, col_q, col_k, col_v)
    rec = _hgrn(proj, proj_meta, hgrn_lb_logits.astype(F32), hgrn_norm_g[0], batch, seq, col_hq,
                _tile(seq, HGRN_ROWS_PER_STEP))

    wo = w_out[0].astype(BF16)
    h2, xnt = _out_proj(x2d, att, rec, wo[:Q_COLS], wo[Q_COLS:], ln_ffn_g[0], _tile(t, OUT_PROJ_TM))

    wqt = jnp.transpose(peer_w_q[0]).astype(BF16)
    keys = peer_sub_keys[0].astype(BF16)
    a1, a2, tau = _peer_keys(xnt, wqt, keys, _tile(t, PEER_TM))

    vt = jnp.transpose(peer_v[0]).astype(BF16)
    out = _peer_dense(xnt, h2, peer_u[0].astype(BF16), vt, a1, a2, tau, _tile(t, PEER_TM), PEER_EXPERT_BLOCK)
    return out.reshape(batch, seq, d)
```

```python
import functools
import math

import jax
import jax.numpy as jnp
import numpy as np
from jax import lax
from jax.experimental import pallas as pl
from jax.experimental.pallas import tpu as pltpu

F32 = jnp.float32
BF16 = jnp.bfloat16

N_META = 16
RMS_EPS = 1e-6
HEAD_DIM = 128
ATT_Q_HEADS = 8
ATT_KV_HEADS = 2
ATT_GROUP = ATT_Q_HEADS // ATT_KV_HEADS
ATT_BLOCK = 128
ROPE_THETA = 500000.0
ROPE_DIM = HEAD_DIM // 4
HGRN_HEADS = 8
HGRN_DK = 128
HGRN_CHUNK = 64
Q_COLS = ATT_Q_HEADS * HEAD_DIM
KV_COLS = ATT_KV_HEADS * HEAD_DIM
HK = HGRN_HEADS * HGRN_DK
IN_COLS = Q_COLS + 2 * KV_COLS + 4 * HK
IN_PROJ_TM = 1024
IN_PROJ_TN = IN_COLS // 4
HGRN_ROWS_PER_STEP = 256
OUT_PROJ_TM = 512
PEER_TM = 1024
PEER_EXPERT_BLOCK = 512
PEER_HEADS = 8
PEER_NKEYS = 128
PEER_DHALF = 128
PEER_TOPK = 16

V7X_VMEM_BYTES = 64 * 1024 * 1024
VMEM_LIMIT = V7X_VMEM_BYTES - 8 * 1024 * 1024
LANES = 128
HGRN_EXP_CLAMP = 85.0
MASKED = -1e30


def _params(*sem):
    return pltpu.CompilerParams(dimension_semantics=sem, vmem_limit_bytes=VMEM_LIMIT)


def _rms(x, g):
    ms = jnp.mean(x * x, axis=-1, keepdims=True)
    return x * lax.rsqrt(ms + RMS_EPS) * g


def _sigmoid(x):
    return 1.0 / (1.0 + jnp.exp(-x))


def _in_proj_kernel(x_ref, g_ref, w_ref, o_ref, xn_ref):
    @pl.when(pl.program_id(1) == 0)
    def _():
        xn_ref[...] = _rms(x_ref[...], g_ref[...]).astype(BF16)

    o_ref[...] = jnp.dot(xn_ref[...], w_ref[...], preferred_element_type=F32)


def _in_proj(x2d, g, w, tm, tn):
    m, d = x2d.shape
    n = w.shape[1]
    return pl.pallas_call(
        _in_proj_kernel,
        grid=(m // tm, n // tn),
        in_specs=[
            pl.BlockSpec((tm, d), lambda i, j: (i, 0)),
            pl.BlockSpec((1, d), lambda i, j: (0, 0)),
            pl.BlockSpec((d, tn), lambda i, j: (0, j)),
        ],
        out_specs=pl.BlockSpec((tm, tn), lambda i, j: (i, j)),
        out_shape=jax.ShapeDtypeStruct((m, n), F32),
        scratch_shapes=[pltpu.VMEM((tm, d), BF16)],
        compiler_params=_params("arbitrary", "arbitrary"),
        name="in_proj",
    )(x2d, g.reshape(1, d), w)


ROPE_HALF = ROPE_DIM // 2
ROPE_LANE_ORDER = np.concatenate([
    np.arange(0, ROPE_HALF), np.arange(ROPE_DIM, ROPE_DIM + HEAD_DIM // 2 - ROPE_HALF),
    np.arange(ROPE_HALF, ROPE_DIM), np.arange(ROPE_DIM + HEAD_DIM // 2 - ROPE_HALF, HEAD_DIM)])


def _rope(x, tab_ref):
    return x * tab_ref[0] + pltpu.roll(x, HEAD_DIM // 2, 1) * tab_ref[1]


def _attn_kernel(sink_ref, q_ref, k_ref, v_ref, km_ref, vm_ref, qg_ref, kg_ref, tab_ref, tabm_ref, o_ref,
                 kprev_s, vprev_s, kmeta_s, vmeta_s):
    b, n = pl.program_id(0), pl.program_id(1)
    blk = ATT_BLOCK
    rows = ATT_GROUP * blk
    scale = HEAD_DIM ** -0.5

    @pl.when(jnp.logical_and(b == 0, n == 0))
    def _():
        pad = jnp.zeros((blk - N_META, HEAD_DIM), BF16)
        for g in range(ATT_KV_HEADS):
            ksl = slice(g * HEAD_DIM, (g + 1) * HEAD_DIM)
            k_meta = _rope(_rms(km_ref[:, ksl], kg_ref[...]), tabm_ref).astype(BF16)
            kmeta_s[g] = jnp.concatenate([k_meta, pad], axis=0)
            vmeta_s[g] = jnp.concatenate([vm_ref[:, ksl].astype(BF16), pad], axis=0)

    @pl.when(n == 0)
    def _():
        kprev_s[...] = jnp.zeros_like(kprev_s)
        vprev_s[...] = jnp.zeros_like(vprev_s)

    qi = lax.broadcasted_iota(jnp.int32, (rows, blk), 0) & (blk - 1)
    ki = lax.broadcasted_iota(jnp.int32, (rows, blk), 1)
    own_mask = ki <= qi
    prev_mask = ki > jnp.where(n > 0, qi, blk)
    meta_mask = ki < N_META
    head_row = lax.shift_right_logical(lax.broadcasted_iota(jnp.int32, (rows, 1), 0), 7)
    nt = (((1,), (1,)), ((), ()))
    for g in range(ATT_KV_HEADS):
        ksl = slice(g * HEAD_DIM, (g + 1) * HEAD_DIM)
        k_own = _rope(_rms(k_ref[:, ksl], kg_ref[...]), tab_ref).astype(BF16)
        v_own = v_ref[:, ksl].astype(BF16)
        keys = jnp.concatenate([kprev_s[g], k_own, kmeta_s[g]], axis=0)
        vals = jnp.concatenate([vprev_s[g], v_own, vmeta_s[g]], axis=0)
        kprev_s[g] = k_own
        vprev_s[g] = v_own
        qs = []
        sink = jnp.zeros((rows, 1), F32)
        for hh in range(ATT_GROUP):
            h = g * ATT_GROUP + hh
            qs.append(_rope(_rms(q_ref[:, h * HEAD_DIM:(h + 1) * HEAD_DIM], qg_ref[...]), tab_ref))
            sink = jnp.where(head_row == hh, sink_ref[h], sink)
        q = jnp.concatenate(qs, axis=0).astype(BF16)
        s = lax.dot_general(q, keys, nt, preferred_element_type=F32) * scale
        s_prev = jnp.where(prev_mask, s[:, :blk], MASKED)
        s_own = jnp.where(own_mask, s[:, blk:2 * blk], MASKED)
        s_meta = jnp.where(meta_mask, s[:, 2 * blk:], MASKED)
        m = jnp.maximum(jnp.max(jnp.maximum(jnp.maximum(s_prev, s_own), s_meta), axis=-1, keepdims=True), sink)
        p_prev, p_own, p_meta = jnp.exp(s_prev - m), jnp.exp(s_own - m), jnp.exp(s_meta - m)
        denom = jnp.sum(p_prev + p_own + p_meta, axis=-1, keepdims=True) + jnp.exp(sink - m)
        p = jnp.concatenate([p_prev, p_own, p_meta], axis=1).astype(BF16)
        out = jnp.dot(p, vals, preferred_element_type=F32) / denom
        for hh in range(ATT_GROUP):
            h = g * ATT_GROUP + hh
            o_ref[:, h * HEAD_DIM:(h + 1) * HEAD_DIM] = out[hh * blk:(hh + 1) * blk].astype(BF16)


def _rope_tables(pos):
    inv = ROPE_THETA ** (-jnp.arange(0, ROPE_DIM, 2, dtype=F32) / ROPE_DIM)
    ang = pos.astype(F32)[:, None] * inv[None, :]
    cos, sin = jnp.cos(ang), jnp.sin(ang)
    n = pos.shape[0]
    gap = HEAD_DIM // 2 - ROPE_HALF
    c = jnp.concatenate([cos, jnp.ones((n, gap), F32), cos, jnp.ones((n, gap), F32)], axis=1)
    s = jnp.concatenate([-sin, jnp.zeros((n, gap), F32), sin, jnp.zeros((n, gap), F32)], axis=1)
    return jnp.stack([c, s])


def _attention(proj, proj_meta, sinks, qg, kg, batch, seq, col_q, col_k, col_v):
    nb = seq // ATT_BLOCK
    t = batch * seq
    tab = _rope_tables(jnp.arange(seq) + N_META)
    tabm = _rope_tables(jnp.arange(N_META))
    kb, vb = col_k // KV_COLS, col_v // KV_COLS
    kv_scratch = pltpu.VMEM((ATT_KV_HEADS, ATT_BLOCK, HEAD_DIM), BF16)
    return pl.pallas_call(
        _attn_kernel,
        grid=(batch, nb),
        in_specs=[
            pl.BlockSpec(memory_space=pltpu.SMEM),
            pl.BlockSpec((ATT_BLOCK, Q_COLS), lambda b, n: (b * nb + n, col_q // Q_COLS)),
            pl.BlockSpec((ATT_BLOCK, KV_COLS), lambda b, n: (b * nb + n, kb)),
            pl.BlockSpec((ATT_BLOCK, KV_COLS), lambda b, n: (b * nb + n, vb)),
            pl.BlockSpec((N_META, KV_COLS), lambda b, n: (0, kb)),
            pl.BlockSpec((N_META, KV_COLS), lambda b, n: (0, vb)),
            pl.BlockSpec((1, HEAD_DIM), lambda b, n: (0, 0)),
            pl.BlockSpec((1, HEAD_DIM), lambda b, n: (0, 0)),
            pl.BlockSpec((2, ATT_BLOCK, HEAD_DIM), lambda b, n: (0, n, 0)),
            pl.BlockSpec((2, N_META, HEAD_DIM), lambda b, n: (0, 0, 0)),
        ],
        out_specs=pl.BlockSpec((ATT_BLOCK, Q_COLS), lambda b, n: (b * nb + n, 0)),
        out_shape=jax.ShapeDtypeStruct((t, Q_COLS), BF16),
        scratch_shapes=[kv_scratch, kv_scratch, kv_scratch, kv_scratch],
        compiler_params=_params("arbitrary", "arbitrary"),
        name="swa_attention",
    )(sinks, proj, proj, proj, proj_meta, proj_meta,
      qg[ROPE_LANE_ORDER].reshape(1, HEAD_DIM), kg[ROPE_LANE_ORDER].reshape(1, HEAD_DIM), tab, tabm)


def _cumsum_rows(x):
    n = x.shape[0]
    row = lax.broadcasted_iota(jnp.int32, x.shape, 0)
    shift = 1
    while shift < n:
        x = x + jnp.where(row >= shift, pltpu.roll(x, shift, 0), 0.0)
        shift *= 2
    return x


def _hgrn_gates(hq, hf, lb):
    f = lb + (1.0 - lb) * _sigmoid(hf)
    return hq * _sigmoid(hq), (1.0 - lb) * _sigmoid(-hf), _cumsum_rows(jnp.log(f))


def _hgrn_needs_direct(hf):
    half = hf.shape[0] // 2
    steep = jnp.maximum(-hf, 0.0)
    bound = half * math.log(2.0) + jnp.maximum(jnp.sum(steep[:half], axis=0, keepdims=True),
                                               jnp.sum(steep[half:], axis=0, keepdims=True))
    return (jnp.max(bound) > HGRN_EXP_CLAMP).astype(jnp.int32)


def _hgrn_direct_scores(hq, hf, lb, sc_ref, bs_ref, ks_ref):
    c = hq.shape[0]
    q, kk, b = _hgrn_gates(hq, hf, lb)
    bs_ref[...] = b
    ks_ref[...] = kk
    col_id = lax.broadcasted_iota(jnp.int32, (c, c), 1)

    def column(s, accs):
        w = q * jnp.exp(jnp.minimum(b - bs_ref[pl.ds(s, 1), :], 0.0)) * ks_ref[pl.ds(s, 1), :]
        return tuple(
            jnp.where(col_id == s, jnp.sum(w[:, h * HGRN_DK:(h + 1) * HGRN_DK], axis=1, keepdims=True), acc)
            for h, acc in enumerate(accs))

    accs = lax.fori_loop(0, c, column, tuple(jnp.zeros((c, c), F32) for _ in range(HGRN_HEADS)))
    for h in range(HGRN_HEADS):
        sc_ref[h] = accs[h]


def _hgrn_chunk(hq, hf, hi, hg, lb, ng, st_ref, sc_ref, direct):
    c = hq.shape[0]
    half = c // 2
    direct = direct == 1
    q, kk, b = _hgrn_gates(hq, hf, lb)
    b_last = b[c - 1:c, :]
    b_mid = b[half - 1:half, :]
    q_mid = (q * jnp.exp(jnp.minimum(b - b_mid, HGRN_EXP_CLAMP))).astype(BF16)
    k_mid = (kk * jnp.exp(jnp.minimum(b_mid - b, HGRN_EXP_CLAMP))).astype(BF16)
    q_in = (q * jnp.exp(b)).astype(BF16)
    k_out = (kk * jnp.exp(b_last - b)).astype(BF16)
    decay = jnp.exp(b_last)
    v = hi.astype(BF16)
    causal = (lax.broadcasted_iota(jnp.int32, (c, c), 0) >= lax.broadcasted_iota(jnp.int32, (c, c), 1))
    nt = (((1,), (1,)), ((), ()))
    tn = (((0,), (0,)), ((), ()))
    outs = []
    for h in range(HGRN_HEADS):
        sl = slice(h * HGRN_DK, (h + 1) * HGRN_DK)
        st = st_ref[h]
        scores = lax.dot_general(q_mid[:, sl], k_mid[:, sl], nt, preferred_element_type=F32)
        scores = jnp.where(causal, jnp.where(direct, sc_ref[h], scores), 0.0).astype(BF16)
        o = (jnp.dot(scores, v[:, sl], preferred_element_type=F32)
             + lax.dot_general(q_in[:, sl], st.astype(BF16), nt, preferred_element_type=F32))
        st_ref[h] = (st * decay[:, sl]
                     + lax.dot_general(v[:, sl], k_out[:, sl], tn, preferred_element_type=F32))
        outs.append(_rms(o, ng[:, sl]))
    o_all = jnp.concatenate(outs, axis=1)
    return o_all * (hg * _sigmoid(hg))


def _lower_bound(lbl_ref):
    l0, l1 = lbl_ref[0:1, :], lbl_ref[1:2, :]
    m = jnp.maximum(l0, l1)
    e0, e1 = jnp.exp(l0 - m), jnp.exp(l1 - m)
    return e0 / (e0 + e1)


def _hgrn_meta_kernel(hq_ref, hf_ref, hi_ref, hg_ref, lbl_ref, ng_ref, st_out_ref, sc_ref, bs_ref, ks_ref):
    st_out_ref[...] = jnp.zeros_like(st_out_ref)
    sc_ref[...] = jnp.zeros_like(sc_ref)
    lb = _lower_bound(lbl_ref)
    direct = _hgrn_needs_direct(hf_ref[...])

    @pl.when(direct == 1)
    def _():
        _hgrn_direct_scores(hq_ref[...], hf_ref[...], lb, sc_ref, bs_ref, ks_ref)

    _hgrn_chunk(hq_ref[...], hf_ref[...], hi_ref[...], hg_ref[...], lb, ng_ref[...], st_out_ref, sc_ref, direct)


def _hgrn_work_buffers(c):
    return [pltpu.VMEM((HGRN_HEADS, c, c), F32), pltpu.VMEM((c, HK), F32), pltpu.VMEM((c, HK), F32)]


def _hgrn_kernel(hq_ref, hf_ref, hi_ref, hg_ref, lbl_ref, ng_ref, st0_ref, o_ref, st_ref, sc_ref, bs_ref, ks_ref,
                 *, chunks):
    @pl.when(pl.program_id(1) == 0)
    def _():
        st_ref[...] = st0_ref[...]
        sc_ref[...] = jnp.zeros_like(sc_ref)

    lb = _lower_bound(lbl_ref)

    def chunk_rows(ci):
        return pl.ds(pl.multiple_of(ci * HGRN_CHUNK, HGRN_CHUNK), HGRN_CHUNK)

    def body(ci, direct):
        rows = chunk_rows(ci)

        @pl.when(direct == 1)
        def _():
            _hgrn_direct_scores(hq_ref[rows, :], hf_ref[rows, :], lb, sc_ref, bs_ref, ks_ref)

        o = _hgrn_chunk(hq_ref[rows, :], hf_ref[rows, :], hi_ref[rows, :], hg_ref[rows, :], lb, ng_ref[...],
                        st_ref, sc_ref, direct)
        o_ref[rows, :] = o.astype(BF16)
        return _hgrn_needs_direct(hf_ref[chunk_rows(jnp.minimum(ci + 1, chunks - 1)), :])

    lax.fori_loop(0, chunks, body, _hgrn_needs_direct(hf_ref[0:HGRN_CHUNK, :]))


def _hgrn(proj, proj_meta, lb_logits, norm_g, batch, seq, col_hq, rows_per_step):
    t = batch * seq
    cb = col_hq // HK
    state_shape = (HGRN_HEADS, HGRN_DK, HGRN_DK)
    vec = pl.BlockSpec((1, HK), lambda *_: (0, 0))
    lbs = pl.BlockSpec((2, HK), lambda *_: (0, 0))
    st_meta = pl.pallas_call(
        _hgrn_meta_kernel,
        grid=(1,),
        in_specs=[pl.BlockSpec((N_META, HK), lambda i, k=k: (0, cb + k)) for k in range(4)] + [lbs, vec],
        out_specs=pl.BlockSpec(state_shape, lambda i: (0, 0, 0)),
        out_shape=jax.ShapeDtypeStruct(state_shape, F32),
        scratch_shapes=_hgrn_work_buffers(N_META),
        compiler_params=_params("arbitrary"),
        name="hgrn2_meta",
    )(proj_meta, proj_meta, proj_meta, proj_meta, lb_logits, norm_g.reshape(1, HK))
    steps = seq // rows_per_step
    return pl.pallas_call(
        functools.partial(_hgrn_kernel, chunks=rows_per_step // HGRN_CHUNK),
        grid=(batch, steps),
        in_specs=[pl.BlockSpec((rows_per_step, HK), lambda b, s, k=k: (b * steps + s, cb + k)) for k in range(4)]
        + [lbs, vec, pl.BlockSpec(state_shape, lambda b, s: (0, 0, 0))],
        out_specs=pl.BlockSpec((rows_per_step, HK), lambda b, s: (b * steps + s, 0)),
        out_shape=jax.ShapeDtypeStruct((t, HK), BF16),
        scratch_shapes=[pltpu.VMEM(state_shape, F32)] + _hgrn_work_buffers(HGRN_CHUNK),
        compiler_params=_params("arbitrary", "arbitrary"),
        name="hgrn2",
    )(proj, proj, proj, proj, lb_logits, norm_g.reshape(1, HK), st_meta)


def _out_proj_kernel(x_ref, att_ref, rec_ref, wa_ref, wr_ref, g_ref, h_ref, xnt_ref):
    h = (x_ref[...]
         + jnp.dot(att_ref[...], wa_ref[...], preferred_element_type=F32)
         + jnp.dot(rec_ref[...], wr_ref[...], preferred_element_type=F32))
    h_ref[...] = h
    xnt_ref[...] = jnp.transpose(_rms(h, g_ref[...])).astype(BF16)


def _out_proj(x2d, att, rec, wa, wr, g, tm):
    t, d = x2d.shape
    return pl.pallas_call(
        _out_proj_kernel,
        grid=(t // tm,),
        in_specs=[
            pl.BlockSpec((tm, d), lambda i: (i, 0)),
            pl.BlockSpec((tm, att.shape[1]), lambda i: (i, 0)),
            pl.BlockSpec((tm, rec.shape[1]), lambda i: (i, 0)),
            pl.BlockSpec(wa.shape, lambda i: (0, 0)),
            pl.BlockSpec(wr.shape, lambda i: (0, 0)),
            pl.BlockSpec((1, d), lambda i: (0, 0)),
        ],
        out_specs=[pl.BlockSpec((tm, d), lambda i: (i, 0)), pl.BlockSpec((d, tm), lambda i: (0, i))],
        out_shape=[jax.ShapeDtypeStruct((t, d), F32), jax.ShapeDtypeStruct((d, t), BF16)],
        compiler_params=_params("arbitrary"),
        name="out_proj",
    )(x2d, att, rec, wa, wr, g.reshape(1, d))


def _sort_desc(v):
    v = list(v)
    n = len(v)
    k = 2
    while k <= n:
        j = k // 2
        while j >= 1:
            for i in range(n):
                l = i ^ j
                if l > i:
                    hi, lo = jnp.maximum(v[i], v[l]), jnp.minimum(v[i], v[l])
                    v[i], v[l] = (hi, lo) if (i & k) == 0 else (lo, hi)
            j //= 2
        k *= 2
    return v


def _merge_desc(v):
    v = list(v)
    j = len(v) // 2
    while j >= 1:
        for i in range(len(v)):
            l = i ^ j
            if l > i:
                v[i], v[l] = jnp.maximum(v[i], v[l]), jnp.minimum(v[i], v[l])
        j //= 2
    return v


def _top_merge(a, b):
    n = len(a)
    return _merge_desc([jnp.maximum(a[i], b[n - 1 - i]) for i in range(n)])


def _top_products(w1, w2):
    k = PEER_TOPK
    rows = [[w1[i] * w2[j] for j in range(k // (i + 1))] for i in range(k // 2)]
    tail = [w1[i] * w2[0] for i in range(k // 2, k)]
    b1 = _merge_desc(rows[1] + tail[::-1])
    b2 = _sort_desc(rows[2] + rows[3] + rows[4] + rows[5] + rows[6])
    t = _top_merge(_top_merge(rows[0], b1), b2)
    t[k - 1] = jnp.maximum(t[k - 1], rows[7][0])
    t[k - 2] = jnp.maximum(t[k - 2], rows[7][1])
    return _merge_desc(t)


def _top16_of_rows(s_ref, c):
    blocks = []
    for b in range(PEER_NKEYS // PEER_TOPK):
        blocks.append(_sort_desc([s_ref[c, (b * PEER_TOPK + k) * 8:(b * PEER_TOPK + k + 1) * 8, :]
                                  for k in range(PEER_TOPK)]))
    while len(blocks) > 1:
        blocks = [_top_merge(blocks[i], blocks[i + 1]) for i in range(0, len(blocks), 2)]
    return blocks[0]


def _peer_keys_kernel(xt_ref, wqt_ref, keys_ref, a1_ref, a2_ref, tau_ref, s_even, s_odd):
    tm = xt_ref.shape[1]
    groups = tm // LANES
    bufs = (s_even, s_odd)
    if groups < 8:
        for s_ref in bufs:
            s_ref[...] = jnp.zeros_like(s_ref)

    def scores(h, slot):
        s_ref = bufs[slot]
        r0 = pl.multiple_of(h * 2 * PEER_DHALF, 2 * PEER_DHALF)
        q = jnp.dot(wqt_ref[pl.ds(r0, 2 * PEER_DHALF), :], xt_ref[...], preferred_element_type=F32).astype(BF16)
        s1 = jnp.dot(keys_ref[h, 0], q[:PEER_DHALF], preferred_element_type=F32)
        s2 = jnp.dot(keys_ref[h, 1], q[PEER_DHALF:], preferred_element_type=F32)
        a2_ref[h] = jnp.exp(s2 - jnp.max(s2, axis=0, keepdims=True))
        for g in range(groups):
            s_ref[0, pl.ds(g, PEER_NKEYS, stride=8), :] = s1[:, g * LANES:(g + 1) * LANES]
            s_ref[1, pl.ds(g, PEER_NKEYS, stride=8), :] = s2[:, g * LANES:(g + 1) * LANES]

    def select(h, slot):
        s_ref = bufs[slot]
        t1 = _top16_of_rows(s_ref, 0)
        t2 = _top16_of_rows(s_ref, 1)
        w1 = [jnp.exp(t - t1[0]) for t in t1]
        w2 = [jnp.exp(t - t2[0]) for t in t2]
        top = _top_products(w1, w2)
        z = top[0]
        for m in top[1:]:
            z = z + m
        inv_z = 1.0 / z
        tau_ref[h] = _top_products([w * inv_z for w in w1], w2)[PEER_TOPK - 1][:groups]
        for k in range(PEER_NKEYS):
            a1_ref[h, k] = (jnp.exp(s_ref[0, k * 8:(k + 1) * 8, :] - t1[0]) * inv_z)[:groups]

    def body(j, carry):
        h = 2 * j
        scores(h, 0)
        scores(h + 1, 1)
        select(h, 0)
        select(h + 1, 1)
        return carry

    lax.fori_loop(0, PEER_HEADS // 2, body, 0)


def _peer_keys(xnt, wqt, keys, tm):
    d, t = xnt.shape
    dq = wqt.shape[0]
    groups = tm // LANES
    return pl.pallas_call(
        _peer_keys_kernel,
        grid=(t // tm,),
        in_specs=[
            pl.BlockSpec((d, tm), lambda i: (0, i)),
            pl.BlockSpec((dq, d), lambda i: (0, 0)),
            pl.BlockSpec(keys.shape, lambda i: (0, 0, 0, 0)),
        ],
        out_specs=[
            pl.BlockSpec((PEER_HEADS, PEER_NKEYS, groups, LANES), lambda i: (0, 0, i, 0)),
            pl.BlockSpec((PEER_HEADS, PEER_NKEYS, tm), lambda i: (0, 0, i)),
            pl.BlockSpec((PEER_HEADS, groups, LANES), lambda i: (0, i, 0)),
        ],
        out_shape=[
            jax.ShapeDtypeStruct((PEER_HEADS, PEER_NKEYS, t // LANES, LANES), F32),
            jax.ShapeDtypeStruct((PEER_HEADS, PEER_NKEYS, t), F32),
            jax.ShapeDtypeStruct((PEER_HEADS, t // LANES, LANES), F32),
        ],
        scratch_shapes=[pltpu.VMEM((2, PEER_NKEYS * 8, LANES), F32)] * 2,
        compiler_params=_params("arbitrary"),
        name="peer_keys",
    )(xnt, wqt, keys)


def _gelu_tanh(x):
    c = math.sqrt(2.0 / math.pi)
    return x * (0.5 * (1.0 + jnp.tanh(c * (x + 0.044715 * (x * x * x)))))


def _peer_dense_kernel(xt_ref, u_ref, vt_ref, a1_ref, a2_ref, tau_ref, h_hbm, o_hbm,
                       acc, fin, a_even, a_odd, sem_in, sem_out, *, chunk):
    i, e = pl.program_id(0), pl.program_id(1)
    n_blocks, n_e = pl.num_programs(0), pl.num_programs(1) - 1
    eb, tm = a_even.shape
    keys_per_step = eb // PEER_NKEYS
    key_base = jnp.maximum(e - 1, 0) * keys_per_step

    def block_rows(blk):
        return pl.ds(pl.multiple_of(blk * tm, tm), tm)

    def load(blk):
        return pltpu.make_async_copy(h_hbm.at[block_rows(blk), :], fin, sem_in)

    def store(blk):
        return pltpu.make_async_copy(fin, o_hbm.at[block_rows(blk), :], sem_out)

    @pl.when(e == 0)
    def _():
        acc[...] = jnp.zeros_like(acc)
        a_odd[...] = jnp.zeros_like(a_odd)

    @pl.when(e == 1)
    def _():
        @pl.when(i > 0)
        def _():
            store(i - 1).wait()
        load(i).start()

    def run(a_write, a_read):
        for q in range(tm // chunk):
            lanes = slice(q * chunk, (q + 1) * chunk)
            a_write[:, lanes] = jnp.dot(u_ref[...], xt_ref[:, lanes], preferred_element_type=F32)
            tiles = []
            for k in range(keys_per_step):
                rows = slice(k * PEER_NKEYS, (k + 1) * PEER_NKEYS)
                cols = []
                for g in range(q * chunk // LANES, (q + 1) * chunk // LANES):
                    glanes = slice(g * LANES, (g + 1) * LANES)
                    gate = jnp.zeros((PEER_NKEYS, LANES), F32)
                    for h in range(PEER_HEADS):
                        p = a1_ref[h, pl.ds(key_base + k, 1), g, :] * a2_ref[h, :, glanes]
                        gate = gate + jnp.where(p >= tau_ref[h, g:g + 1, :], p, 0.0)
                    cols.append((_gelu_tanh(a_read[rows, glanes]) * gate).astype(BF16))
                tiles.append(jnp.concatenate(cols, axis=1))
            half = len(tiles) // 2
            h_lo = jnp.concatenate(tiles[:half], axis=0)
            h_hi = jnp.concatenate(tiles[half:], axis=0)
            acc[:, lanes] += (jnp.dot(vt_ref[:, :eb // 2], h_lo, preferred_element_type=F32)
                              + jnp.dot(vt_ref[:, eb // 2:], h_hi, preferred_element_type=F32))

    @pl.when(e % 2 == 0)
    def _():
        run(a_even, a_odd)

    @pl.when(e % 2 == 1)
    def _():
        run(a_odd, a_even)

    @pl.when(e == n_e)
    def _():
        load(i).wait()
        for q in range(tm // chunk):
            lanes = slice(q * chunk, (q + 1) * chunk)
            fin[lanes, :] += jnp.transpose(acc[:, lanes])
        store(i).start()

        @pl.when(i == n_blocks - 1)
        def _():
            store(i).wait()


def _peer_dense(xnt, h2, u, vt, a1, a2, tau, tm, eb):
    d, t = xnt.shape
    n_e = u.shape[0] // eb
    groups = tm // LANES
    return pl.pallas_call(
        functools.partial(_peer_dense_kernel, chunk=min(tm, 256)),
        grid=(t // tm, n_e + 1),
        in_specs=[
            pl.BlockSpec((d, tm), lambda i, e: (0, i)),
            pl.BlockSpec((eb, d), lambda i, e: (jnp.minimum(e, n_e - 1), 0)),
            pl.BlockSpec((d, eb), lambda i, e: (0, jnp.maximum(e - 1, 0))),
            pl.BlockSpec((PEER_HEADS, PEER_NKEYS, groups, LANES), lambda i, e: (0, 0, i, 0)),
            pl.BlockSpec((PEER_HEADS, PEER_NKEYS, tm), lambda i, e: (0, 0, i)),
            pl.BlockSpec((PEER_HEADS, groups, LANES), lambda i, e: (0, i, 0)),
            pl.BlockSpec(memory_space=pl.ANY),
        ],
        out_specs=pl.BlockSpec(memory_space=pl.ANY),
        out_shape=jax.ShapeDtypeStruct((t, d), F32),
        scratch_shapes=[pltpu.VMEM((d, tm), F32), pltpu.VMEM((tm, d), F32),
                        pltpu.VMEM((eb, tm), F32), pltpu.VMEM((eb, tm), F32),
                        pltpu.SemaphoreType.DMA(()), pltpu.SemaphoreType.DMA(())],
        compiler_params=_params("arbitrary", "arbitrary"),
        name="peer_dense",
    )(xnt, u, vt, a1, a2, tau, h2)


def _tile(n, want):
    t = min(n, want)
    assert n % t == 0, (n, want)
    return t


def kernel(x, meta_tokens, hgrn_lb_logits, ln_mix_g, w_in, q_norm_g, k_norm_g, attn_sinks, hgrn_norm_g,
           w_out, ln_ffn_g, peer_w_q, peer_sub_keys, peer_u, peer_v):
    batch, seq, d = x.shape
    assert w_in.shape[0] == 1 and hgrn_lb_logits.shape[0] == 2, "single-layer trunk"
    assert seq % ATT_BLOCK == 0 and peer_sub_keys.shape[1:] == (PEER_HEADS, 2, PEER_NKEYS, PEER_DHALF)
    t = batch * seq
    x2d = x.reshape(t, d)

    wi = w_in[0].astype(BF16)
    o_q, o_k, o_v, o_hq = 0, Q_COLS, Q_COLS + KV_COLS, Q_COLS + 2 * KV_COLS
    lane_perm = jnp.asarray(np.eye(HEAD_DIM, dtype=np.float32)[:, ROPE_LANE_ORDER], BF16)
    qk_heads = wi[:, o_q:o_v].reshape(d, ATT_Q_HEADS + ATT_KV_HEADS, HEAD_DIM)
    qk = jnp.einsum("dhc,ce->dhe", qk_heads, lane_perm, preferred_element_type=F32).astype(BF16)
    qk = qk.reshape(d, Q_COLS + KV_COLS)
    w_perm = jnp.concatenate([qk[:, :Q_COLS], wi[:, o_hq:], qk[:, Q_COLS:], wi[:, o_v:o_hq]], axis=1)
    col_q, col_hq, col_k, col_v = 0, Q_COLS, Q_COLS + 4 * HK, Q_COLS + 4 * HK + KV_COLS

    proj = _in_proj(x2d, ln_mix_g[0], w_perm, _tile(t, IN_PROJ_TM), IN_PROJ_TN)
    proj_meta = _in_proj(meta_tokens.astype(F32), ln_mix_g[0], w_perm, N_META, IN_PROJ_TN)

    att = _attention(proj, proj_meta, attn_sinks[0].astype(F32), q_norm_g[0], k_norm_g[0],
                     batch, seq, col_q, col_k, col_v)
    rec = _hgrn(proj, proj_meta, hgrn_lb_logits.astype(F32), hgrn_norm_g[0], batch, seq, col_hq,
                _tile(seq, HGRN_ROWS_PER_STEP))

    wo = w_out[0].astype(BF16)
    h2, xnt = _out_proj(x2d, att, rec, wo[:Q_COLS], wo[Q_COLS:], ln_ffn_g[0], _tile(t, OUT_PROJ_TM))

    wqt = jnp.transpose(peer_w_q[0]).astype(BF16)
    keys = peer_sub_keys[0].astype(BF16)
    a1, a2, tau = _peer_keys(xnt, wqt, keys, _tile(t, PEER_TM))

    vt = jnp.transpose(peer_v[0]).astype(BF16)
    out = _peer_dense(xnt, h2, peer_u[0].astype(BF16), vt, a1, a2, tau, _tile(t, PEER_TM), PEER_EXPERT_BLOCK)
    return out.reshape(batch, seq, d)
```
